```python
import math
import jax, jax.numpy as jnp
from jax import lax
import numpy as np

D_MODEL = 1024
BATCH = 2
SEQ = 8192
DEPTH = 4
DEC_BATCH = 128
DEC_SEQ = 1
PAST_LEN = 8192
PAGE_SIZE = 128

N_HEADS = 16
HEAD_DIM = 64
N_KV_HEADS = 2
GROUP = N_HEADS // N_KV_HEADS
HD = N_HEADS * HEAD_DIM
KVD = N_KV_HEADS * HEAD_DIM
N_A_LAYERS = DEPTH // 2
N_B_LAYERS = DEPTH - N_A_LAYERS
WINDOW_A = 128
WINDOW_B = 512
Q_BLOCK = 128
CMP_BLOCK = 32
SEL_BLOCK = 64
CMP_PER_SEL = SEL_BLOCK // CMP_BLOCK
SEL_TOPK = 16
CMP_HIDDEN = 2 * HEAD_DIM
N_BRANCH = 3
N_BUCKETS = 32
T5_MAX_DISTANCE = 128
RMS_EPS = 1e-6
ATTN_SCALE = HEAD_DIM ** -0.5

kernel_name = 'yoco_swa_sink_nsa_adaln_decoder_step'


def rmsnorm(x, g):
    xf = x.astype(jnp.float32)
    y = xf * lax.rsqrt(jnp.mean(xf * xf, axis=-1, keepdims=True) + RMS_EPS)
    return (y * g.astype(jnp.float32)).astype(x.dtype)


def ada_params(c, w, b, n_parts):
    h = jax.nn.silu(c) @ w + b
    return [t[:, None, :] for t in jnp.split(h, n_parts, axis=-1)]


def modulate(x, g, shift, scale):
    return rmsnorm(x, g) * (1 + scale) + shift


def t5_bucket(dist):
    n = jnp.maximum(dist, 0)
    exact = N_BUCKETS // 2
    nf = jnp.maximum(n, exact).astype(jnp.float32)
    large = exact + (jnp.log(nf / exact) / math.log(T5_MAX_DISTANCE / exact) * (N_BUCKETS - exact)).astype(jnp.int32)
    return jnp.where(n < exact, n, jnp.minimum(large, N_BUCKETS - 1))


def t5_bias(table, dist):
    b = table.astype(jnp.float32)[t5_bucket(dist)]
    nq, nl = dist.shape
    return jnp.transpose(b, (2, 0, 1)).reshape(N_KV_HEADS, GROUP, nq, nl)


def masked_softmax(logits, valid, sink=None):
    logits = jnp.where(valid, logits, -jnp.inf)
    m = jnp.max(logits, axis=-1, keepdims=True)
    if sink is not None:
        m = jnp.maximum(m, sink)
    m = jnp.where(jnp.isfinite(m), m, 0.0)
    e = jnp.exp(logits - m)
    den = jnp.sum(e, axis=-1, keepdims=True)
    if sink is not None:
        den = den + jnp.exp(sink - m)
    return e / jnp.where(den > 0, den, 1.0)


def gqa_attend(q, k, v, dist, valid, table, sink=None):
    logits = jnp.einsum('nqgrd,nlgd->ngrql', q, k).astype(jnp.float32) * ATTN_SCALE + t5_bias(table, dist)
    p = masked_softmax(logits, valid, sink)
    return jnp.einsum('ngrql,nlgd->nqgrd', p.astype(v.dtype), v)


def window_attend(q, qpos, k, v, kpos, window, table, sink=None):
    dist = qpos[:, None] - kpos[None, :]
    valid = (dist >= 0) & (dist <= window) & (kpos[None, :] >= 0)
    return gqa_attend(q, k, v, dist, valid, table, sink)


def unblock(o):
    return jnp.moveaxis(o, 0, 1).reshape((o.shape[1], o.shape[0] * o.shape[2]) + o.shape[3:])


def window_attn_prompt(q, k, v, window, table, sink=None):
    s = q.shape[1]
    pad = ((0, 0), (window, 0), (0, 0), (0, 0))
    kp, vp = jnp.pad(k, pad), jnp.pad(v, pad)

    def block(b):
        s0 = b * Q_BLOCK
        qb = lax.dynamic_slice_in_dim(q, s0, Q_BLOCK, axis=1)
        kb = lax.dynamic_slice_in_dim(kp, s0, window + Q_BLOCK, axis=1)
        vb = lax.dynamic_slice_in_dim(vp, s0, window + Q_BLOCK, axis=1)
        qpos = s0 + jnp.arange(Q_BLOCK)
        kpos = s0 - window + jnp.arange(window + Q_BLOCK)
        return window_attend(qb, qpos, kb, vb, kpos, window, table, sink)

    return unblock(lax.map(block, jnp.arange(s // Q_BLOCK)))


def window_attn_sample(q, kv_new, buf, window, table, sink=None):
    lb, ds = buf.shape[1], q.shape[1]
    keys = jnp.concatenate([buf, kv_new], axis=1)
    qpos = PAST_LEN + jnp.arange(ds)
    kpos = PAST_LEN - lb + jnp.arange(lb + ds)
    o = window_attend(q, qpos, keys[:, :, 0], keys[:, :, 1], kpos, window, table, sink)
    return o, keys[:, -min(window, lb + ds):]


def a_project(x, c, g, w_ada, b_ada, w_in):
    n, t = x.shape[:2]
    shift, scale, gate = ada_params(c, w_ada, b_ada, 3)
    h = modulate(x, g, shift, scale)
    q, k, v, z = jnp.split(h @ w_in, [HD, HD + KVD, HD + 2 * KVD], axis=-1)
    q = q.reshape(n, t, N_KV_HEADS, GROUP, HEAD_DIM)
    kv = jnp.stack([k, v], axis=2).reshape(n, t, 2, N_KV_HEADS, HEAD_DIM)
    return q, kv, z, gate


def a_finish(x, o, z, gate, w_out):
    n, t = x.shape[:2]
    return x + gate * ((o.reshape(n, t, HD) * jax.nn.silu(z)) @ w_out)


def kv_side(s, c, g, w_ada, b_ada, w_in):
    n, t = s.shape[:2]
    shift, scale = ada_params(c, w_ada, b_ada, 2)
    h = modulate(s, g, shift, scale)
    rows = (h @ w_in).reshape(n, t, N_BRANCH, 2, N_KV_HEADS, HEAD_DIM)
    return rows[:, :, 0], rows[:, :, 1], rows[:, :, 2]


def pad_rows(rows, mult):
    extra = (-rows.shape[1]) % mult
    return jnp.pad(rows, ((0, 0), (0, extra), (0, 0), (0, 0), (0, 0)))


def compress(rows, pos, w1, w2):
    n, l = rows.shape[:2]
    blk = rows.reshape(n, l // CMP_BLOCK, CMP_BLOCK, 2, N_KV_HEADS, HEAD_DIM) + pos[None, None, :, :, None, :]
    hid = jax.nn.silu(jnp.einsum('ncbkgd,kbdh->nckgh', blk, w1))
    return jnp.einsum('nckgh,khd->nckgd', hid, w2)


def prompt_gather(sel_rows):
    n, s = sel_rows.shape[:2]
    blocks = sel_rows.reshape(n, s // SEL_BLOCK, SEL_BLOCK, 2, N_KV_HEADS, HEAD_DIM)
    nidx = jnp.arange(n)[:, None, None, None]
    gidx = jnp.arange(N_KV_HEADS)[None, :, None, None]

    def gather(idx):
        return blocks[nidx, idx, :, :, gidx]
    return gather


def paged_gather(pool, page_table, new_rows):
    n = page_table.shape[0]
    bpp = PAGE_SIZE // SEL_BLOCK
    n_past = PAST_LEN // SEL_BLOCK
    pool_b = pool.reshape(pool.shape[0], bpp, SEL_BLOCK, 2, N_KV_HEADS, HEAD_DIM)
    new_b = pad_rows(new_rows, SEL_BLOCK).reshape(n, -1, SEL_BLOCK, 2, N_KV_HEADS, HEAD_DIM)
    n_new = new_b.shape[1]
    nidx = jnp.arange(n)[:, None, None, None]
    gidx = jnp.arange(N_KV_HEADS)[None, :, None, None]

    def gather(idx):
        past = jnp.minimum(idx, n_past - 1)
        phys = page_table[nidx, past // bpp]
        from_pool = pool_b[phys, past % bpp, :, :, gidx]
        from_new = new_b[nidx, jnp.clip(idx - n_past, 0, n_new - 1), :, :, gidx]
        return jnp.where((idx >= n_past)[..., None, None, None], from_new, from_pool)
    return gather


def nsa_branches(q, qpos, ck, cv, gather_sel, wk, wv, wpos, table):
    n, nq = q.shape[:2]
    nc = ck.shape[1]
    ns = nc // CMP_PER_SEL
    cend = jnp.arange(nc) * CMP_BLOCK + (CMP_BLOCK - 1)
    cvalid = cend[None, :] <= qpos[:, None]
    logits = jnp.einsum('nqgrd,ncgd->ngrqc', q, ck).astype(jnp.float32) * ATTN_SCALE
    p_cmp = masked_softmax(logits, cvalid)
    o_cmp = jnp.einsum('ngrqc,ncgd->nqgrd', p_cmp.astype(cv.dtype), cv)
    imp = p_cmp.reshape(n, N_KV_HEADS, GROUP, nq, ns, CMP_PER_SEL).sum(axis=(2, 5))
    cur = qpos // SEL_BLOCK
    j = jnp.arange(ns)
    forced = (j[None, :] == 0) | (j[None, :] == cur[:, None]) | (j[None, :] == cur[:, None] - 1)
    allowed = j[None, :] <= cur[:, None]
    score = jnp.where(forced, jnp.inf, jnp.where(allowed, imp, -jnp.inf))
    top, idx = lax.top_k(score, min(SEL_TOPK, ns))
    kv_sel = gather_sel(idx)
    kpos = idx[..., None] * SEL_BLOCK + jnp.arange(SEL_BLOCK)
    svalid = (top > -jnp.inf)[..., None] & (kpos <= qpos[None, None, :, None, None])
    nl = idx.shape[-1] * SEL_BLOCK
    ks = kv_sel[..., 0, :].reshape(n, N_KV_HEADS, nq, nl, HEAD_DIM)
    vs = kv_sel[..., 1, :].reshape(n, N_KV_HEADS, nq, nl, HEAD_DIM)
    kpos = kpos.reshape(n, N_KV_HEADS, nq, nl)
    svalid = svalid.reshape(n, N_KV_HEADS, nq, nl)
    gidx = jnp.arange(N_KV_HEADS)[None, :, None, None]
    table3 = table.astype(jnp.float32).reshape(N_BUCKETS, N_KV_HEADS, GROUP)
    bias = table3[t5_bucket(qpos[None, None, :, None] - kpos), gidx]
    logits = jnp.einsum('nqgrd,ngqld->ngqrl', q, ks).astype(jnp.float32) * ATTN_SCALE + jnp.swapaxes(bias, -1, -2)
    p_sel = masked_softmax(logits, svalid[:, :, :, None, :])
    o_sel = jnp.einsum('ngqrl,ngqld->nqgrd', p_sel.astype(vs.dtype), vs)
    o_win = window_attend(q, qpos, wk, wv, wpos, WINDOW_B, table)
    return o_cmp, o_sel, o_win


def nsa_prompt(q, ck, cv, gather_sel, win_rows, table):
    s = q.shape[1]
    wp = jnp.pad(win_rows, ((0, 0), (WINDOW_B, 0), (0, 0), (0, 0), (0, 0)))

    def block(b):
        s0 = b * Q_BLOCK
        qb = lax.dynamic_slice_in_dim(q, s0, Q_BLOCK, axis=1)
        wb = lax.dynamic_slice_in_dim(wp, s0, WINDOW_B + Q_BLOCK, axis=1)
        qpos = s0 + jnp.arange(Q_BLOCK)
        wpos = s0 - WINDOW_B + jnp.arange(WINDOW_B + Q_BLOCK)
        return nsa_branches(qb, qpos, ck, cv, gather_sel, wb[:, :, 0], wb[:, :, 1], wpos, table)

    o_cmp, o_sel, o_win = lax.map(block, jnp.arange(s // Q_BLOCK))
    return unblock(o_cmp), unblock(o_sel), unblock(o_win)


def b_project(x, c, g, w_ada, b_ada, w_in):
    n, t = x.shape[:2]
    shift, scale, gate = ada_params(c, w_ada, b_ada, 3)
    h = modulate(x, g, shift, scale)
    q, z, gl = jnp.split(h @ w_in, [HD, HD + N_BRANCH * HD], axis=-1)
    q = q.reshape(n, t, N_KV_HEADS, GROUP, HEAD_DIM)
    z = z.reshape(n, t, N_BRANCH, N_HEADS, HEAD_DIM)
    gl = gl.reshape(n, t, N_BRANCH, N_HEADS)
    return q, z, gl, gate


def b_finish(x, o_cmp, o_sel, o_win, z, gl, gate, w_out):
    n, t = x.shape[:2]
    ob = jnp.stack([o_cmp, o_sel, o_win], axis=2).reshape(n, t, N_BRANCH, N_HEADS, HEAD_DIM)
    mix = jnp.sum(jax.nn.sigmoid(gl)[..., None] * ob * jax.nn.silu(z), axis=2)
    return x + gate * (mix.reshape(n, t, HD) @ w_out)


def setup_inputs(seed: int = 0) -> dict:
    key = jax.random.key(seed)
    ks = jax.random.split(key, 32)
    f32 = jnp.float32

    def nrm(k, shape, scale):
        return scale * jax.random.normal(k, shape, f32)

    n_pages = PAST_LEN // PAGE_SIZE
    n_phys = (5 * DEC_BATCH * n_pages) // 4
    lba = min(WINDOW_A, PAST_LEN)
    lbb = min(WINDOW_B, PAST_LEN)
    page_table = jax.random.permutation(ks[8], n_phys)[: DEC_BATCH * n_pages].reshape(DEC_BATCH, n_pages).astype(jnp.int32)
    d_in_a = HD + 2 * KVD + HD
    d_in_b = HD + N_BRANCH * HD + N_BRANCH * N_HEADS
    return {
        'x_prompt': nrm(ks[0], (BATCH, SEQ, D_MODEL), 1.0),
        'x_sample': nrm(ks[1], (DEC_BATCH, DEC_SEQ, D_MODEL), 1.0),
        'state_a_kv': nrm(ks[2], (N_A_LAYERS, DEC_BATCH, lba, 2, N_KV_HEADS, HEAD_DIM), 1.0),
        'cache_cmp_kv': nrm(ks[3], (n_phys, PAGE_SIZE, 2, N_KV_HEADS, HEAD_DIM), 1.0),
        'cache_sel_kv': nrm(ks[4], (n_phys, PAGE_SIZE, 2, N_KV_HEADS, HEAD_DIM), 1.0),
        'state_win_kv': nrm(ks[5], (DEC_BATCH, lbb, 2, N_KV_HEADS, HEAD_DIM), 1.0),
        'page_table': page_table,
        'c_prompt': nrm(ks[6], (BATCH, D_MODEL), 1.0),
        'c_sample': nrm(ks[7], (DEC_BATCH, D_MODEL), 1.0),
        'rel_bias_table': nrm(ks[9], (N_BUCKETS, N_HEADS), 0.2),
        'a_norm_g': 1.0 + nrm(ks[10], (N_A_LAYERS, D_MODEL), 0.02),
        'a_w_ada': nrm(ks[11], (N_A_LAYERS, D_MODEL, 3 * D_MODEL), 0.5 * D_MODEL ** -0.5),
        'a_b_ada': nrm(ks[12], (N_A_LAYERS, 3 * D_MODEL), 0.02),
        'a_w_in': nrm(ks[13], (N_A_LAYERS, D_MODEL, d_in_a), D_MODEL ** -0.5),
        'a_sinks': nrm(ks[14], (N_A_LAYERS, N_HEADS), 0.5),
        'a_w_out': nrm(ks[15], (N_A_LAYERS, HD, D_MODEL), HD ** -0.5),
        'kv_norm_g': 1.0 + nrm(ks[16], (D_MODEL,), 0.02),
        'kv_w_ada': nrm(ks[17], (D_MODEL, 2 * D_MODEL), 0.5 * D_MODEL ** -0.5),
        'kv_b_ada': nrm(ks[18], (2 * D_MODEL,), 0.02),
        'kv_w_in': nrm(ks[19], (D_MODEL, N_BRANCH * 2 * KVD), D_MODEL ** -0.5),
        'cmp_pos': nrm(ks[20], (CMP_BLOCK, 2, HEAD_DIM), 0.1),
        'cmp_w1': nrm(ks[21], (2, CMP_BLOCK, HEAD_DIM, CMP_HIDDEN), (CMP_BLOCK * HEAD_DIM) ** -0.5),
        'cmp_w2': nrm(ks[22], (2, CMP_HIDDEN, HEAD_DIM), CMP_HIDDEN ** -0.5),
        'b_norm_g': 1.0 + nrm(ks[23], (N_B_LAYERS, D_MODEL), 0.02),
        'b_w_ada': nrm(ks[24], (N_B_LAYERS, D_MODEL, 3 * D_MODEL), 0.5 * D_MODEL ** -0.5),
        'b_b_ada': nrm(ks[25], (N_B_LAYERS, 3 * D_MODEL), 0.02),
        'b_w_in': nrm(ks[26], (N_B_LAYERS, D_MODEL, d_in_b), D_MODEL ** -0.5),
        'b_w_out': nrm(ks[27], (N_B_LAYERS, HD, D_MODEL), HD ** -0.5),
        'final_norm_g': 1.0 + nrm(ks[28], (D_MODEL,), 0.02),
    }


def reference(x_prompt, x_sample, state_a_kv, cache_cmp_kv, cache_sel_kv, state_win_kv, page_table, c_prompt, c_sample, rel_bias_table, a_norm_g, a_w_ada, a_b_ada, a_w_in, a_sinks, a_w_out, kv_norm_g, kv_w_ada, kv_b_ada, kv_w_in, cmp_pos, cmp_w1, cmp_w2, b_norm_g, b_w_ada, b_b_ada, b_w_in, b_w_out, final_norm_g):
    xp, xs = x_prompt, x_sample
    n_dec, ds = xs.shape[:2]
    qpos_s = PAST_LEN + jnp.arange(ds)
    a_new_p, a_new_s = [], []
    for layer in range(DEPTH):
        if layer < N_A_LAYERS:
            i = layer
            sink = a_sinks[i].astype(jnp.float32).reshape(N_KV_HEADS, GROUP, 1, 1)
            q, kv, z, gate = a_project(xp, c_prompt, a_norm_g[i], a_w_ada[i], a_b_ada[i], a_w_in[i])
            o = window_attn_prompt(q, kv[:, :, 0], kv[:, :, 1], WINDOW_A, rel_bias_table, sink)
            xp = a_finish(xp, o, z, gate, a_w_out[i])
            a_new_p.append(kv[:, -min(WINDOW_A, kv.shape[1]):])
            q, kv, z, gate = a_project(xs, c_sample, a_norm_g[i], a_w_ada[i], a_b_ada[i], a_w_in[i])
            o, buf = window_attn_sample(q, kv, state_a_kv[i], WINDOW_A, rel_bias_table, sink)
            xs = a_finish(xs, o, z, gate, a_w_out[i])
            a_new_s.append(buf)
        else:
            if layer == N_A_LAYERS:
                cmp_p, sel_p, win_p = kv_side(xp, c_prompt, kv_norm_g, kv_w_ada, kv_b_ada, kv_w_in)
                cmp_s, sel_s, win_s = kv_side(xs, c_sample, kv_norm_g, kv_w_ada, kv_b_ada, kv_w_in)
                comp_p = compress(pad_rows(cmp_p, SEL_BLOCK), cmp_pos, cmp_w1, cmp_w2)
                past_cmp = cache_cmp_kv[page_table].reshape(n_dec, page_table.shape[1] * PAGE_SIZE, 2, N_KV_HEADS, HEAD_DIM)
                full_cmp = pad_rows(jnp.concatenate([past_cmp, cmp_s], axis=1), SEL_BLOCK)
                comp_s = compress(full_cmp, cmp_pos, cmp_w1, cmp_w2)
                gather_p = prompt_gather(sel_p)
                gather_s = paged_gather(cache_sel_kv, page_table, sel_s)
                lbb = state_win_kv.shape[1]
                win_keys_s = jnp.concatenate([state_win_kv, win_s], axis=1)
                wpos_s = PAST_LEN - lbb + jnp.arange(lbb + ds)
                new_win_kv_sample = win_keys_s[:, -min(WINDOW_B, lbb + ds):]
            j = layer - N_A_LAYERS
            q, z, gl, gate = b_project(xp, c_prompt, b_norm_g[j], b_w_ada[j], b_b_ada[j], b_w_in[j])
            o_cmp, o_sel, o_win = nsa_prompt(q, comp_p[:, :, 0], comp_p[:, :, 1], gather_p, win_p, rel_bias_table)
            xp = b_finish(xp, o_cmp, o_sel, o_win, z, gl, gate, b_w_out[j])
            q, z, gl, gate = b_project(xs, c_sample, b_norm_g[j], b_w_ada[j], b_b_ada[j], b_w_in[j])
            o_cmp, o_sel, o_win = nsa_branches(q, qpos_s, comp_s[:, :, 0], comp_s[:, :, 1], gather_s, win_keys_s[:, :, 0], win_keys_s[:, :, 1], wpos_s, rel_bias_table)
            xs = b_finish(xs, o_cmp, o_sel, o_win, z, gl, gate, b_w_out[j])
    y_prompt = rmsnorm(xp, final_norm_g)
    y_sample = rmsnorm(xs, final_norm_g)
    new_a_kv_prompt = jnp.stack(a_new_p)
    new_a_kv_sample = jnp.stack(a_new_s)
    new_win_kv_prompt = win_p[:, -min(WINDOW_B, win_p.shape[1]):]
    return (y_prompt, y_sample, new_a_kv_prompt, new_a_kv_sample, cmp_p, cmp_s, sel_p, sel_s, new_win_kv_prompt, new_win_kv_sample)
```

```python
import functools
import math

import numpy as np
import jax
import jax.numpy as jnp
from jax import lax
from jax.experimental import pallas as pl
from jax.experimental.pallas import tpu as pltpu

F32, BF16 = jnp.float32, jnp.bfloat16

D_MODEL = 1024
N_HEADS = 16
HEAD_DIM = 64
N_KV_HEADS = 2
GROUP = N_HEADS // N_KV_HEADS
HD = N_HEADS * HEAD_DIM
KVD = N_KV_HEADS * HEAD_DIM
KV_ROW = 2 * KVD
WINDOW_A = 128
WINDOW_B = 512
Q_BLOCK = 128
CMP_BLOCK = 32
SEL_BLOCK = 64
SEL_TOPK = 16
CMP_HIDDEN = 2 * HEAD_DIM
N_BRANCH = 3
N_BUCKETS = 32
T5_MAX_DISTANCE = 128
RMS_EPS = 1e-6
ATTN_SCALE = HEAD_DIM ** -0.5
PAGE_SIZE = 128
LANES = 128
NEG = -1e30
VMEM_LIMIT = 56 * 1024 * 1024

_PERM = np.array([(GROUP * g + c) * HEAD_DIM + d for c in range(GROUP) for g in range(N_KV_HEADS)
                  for d in range(HEAD_DIM)], dtype=np.int32)


def _params(sem):
    return pltpu.CompilerParams(dimension_semantics=sem, vmem_limit_bytes=VMEM_LIMIT)


def _silu(x):
    return x * jax.nn.sigmoid(x)


def _ada_kernel(c_ref, w_ref, b_ref, o_ref):
    a = _silu(c_ref[...]).astype(BF16)
    o_ref[0] = jnp.dot(a, w_ref[0].astype(BF16), preferred_element_type=F32) + b_ref[0]


def _ada(c_all, w, b):
    nl, d, n = w.shape
    m = c_all.shape[0]
    tn = 512
    return pl.pallas_call(
        _ada_kernel,
        grid=(nl, n // tn),
        in_specs=[pl.BlockSpec((m, d), lambda l, j: (0, 0)),
                  pl.BlockSpec((1, d, tn), lambda l, j: (l, 0, j)),
                  pl.BlockSpec((1, 1, tn), lambda l, j: (l, 0, j))],
        out_specs=pl.BlockSpec((1, m, tn), lambda l, j: (l, 0, j)),
        out_shape=jax.ShapeDtypeStruct((nl, m, n), F32),
        compiler_params=_params(("parallel", "parallel")),
        name="ada_params",
    )(c_all, w, b.reshape(nl, 1, n))


def _mod_proj_kernel(x_ref, g_ref, sh_ref, sc_ref, *refs, nw, outs):
    w_refs, o_refs = refs[:nw], refs[nw:]
    x = x_ref[0]
    y = x * lax.rsqrt(jnp.mean(x * x, axis=-1, keepdims=True) + RMS_EPS) * g_ref[...]
    h = (y * (1.0 + sc_ref[0]) + sh_ref[0]).astype(BF16)
    res = {}
    for (wi, c0, c1, dt), o_ref in zip(outs, o_refs):
        if wi not in res:
            res[wi] = jnp.dot(h, w_refs[wi][...], preferred_element_type=F32)
        o_ref[0] = res[wi][:, c0:c1].astype(dt)


def _mod_proj(x, g, shift, scale, ws, outs, tm):
    nb, t, d = x.shape
    tm = min(tm, t)
    if shift.shape[1] == 1:
        mod_spec = pl.BlockSpec((1, 1, d), lambda n, i: (n, 0, 0))
    else:
        mod_spec = pl.BlockSpec((1, tm, d), lambda n, i: (n, i, 0))
    in_specs = [pl.BlockSpec((1, tm, d), lambda n, i: (n, i, 0)),
                pl.BlockSpec((1, d), lambda n, i: (0, 0)), mod_spec, mod_spec]
    in_specs += [pl.BlockSpec(w.shape, lambda n, i: (0, 0)) for w in ws]
    out_specs = [pl.BlockSpec((1, tm, c1 - c0), lambda n, i: (n, i, 0)) for (_, c0, c1, _) in outs]
    out_shape = [jax.ShapeDtypeStruct((nb, t, c1 - c0), dt) for (_, c0, c1, dt) in outs]
    return pl.pallas_call(
        functools.partial(_mod_proj_kernel, nw=len(ws), outs=tuple(outs)),
        grid=(nb, t // tm),
        in_specs=in_specs, out_specs=out_specs, out_shape=out_shape,
        compiler_params=_params(("parallel", "parallel")),
        name="mod_proj",
    )(x, g.reshape(1, d), shift, scale, *ws)


def _stack_queries(q_ref, qs_ref, tq):
    lane = lax.broadcasted_iota(jnp.int32, (tq, LANES), 1)
    for c in range(GROUP):
        qc = (q_ref[0, :, LANES * c:LANES * (c + 1)].astype(F32) * ATTN_SCALE)
        qs_ref[0, c * tq:(c + 1) * tq, :] = jnp.where(lane < HEAD_DIM, qc, 0.0).astype(BF16)
        qs_ref[1, c * tq:(c + 1) * tq, :] = jnp.where(lane >= HEAD_DIM, qc, 0.0).astype(BF16)


def _store_paired(o_ref, o0, o1, tq):
    lane = lax.broadcasted_iota(jnp.int32, o0.shape, 1)
    oc = jnp.where(lane < HEAD_DIM, o0, o1)
    for c in range(GROUP):
        o_ref[0, :, LANES * c:LANES * (c + 1)] = oc[c * tq:(c + 1) * tq].astype(o_ref.dtype)


def _nt_dot(a, b):
    return lax.dot_general(a, b, (((1,), (1,)), ((), ())), preferred_element_type=F32)


def _band_attn_kernel(*refs, tq, window, use_sink):
    if use_sink:
        q_ref, k_ref, v_ref, bias_ref, sink_ref, o_ref, qs_ref = refs
    else:
        q_ref, k_ref, v_ref, bias_ref, o_ref, qs_ref = refs
    s0 = pl.multiple_of(pl.program_id(1) * tq, tq)
    band = window + tq
    _stack_queries(q_ref, qs_ref, tq)
    kb = k_ref[0, pl.ds(s0, band), :]
    vb = v_ref[0, pl.ds(s0, band), :]
    kpos = s0 - window + lax.broadcasted_iota(jnp.int32, (1, band), 1)
    front = jnp.where(kpos >= 0, 0.0, NEG)
    outs = []
    for g in range(N_KV_HEADS):
        s = _nt_dot(qs_ref[g], kb) + bias_ref[g] + front
        m = jnp.max(s, axis=-1, keepdims=True)
        if use_sink:
            m = jnp.maximum(m, sink_ref[g])
        e = jnp.exp(s - m)
        den = jnp.sum(e, axis=-1, keepdims=True)
        if use_sink:
            den = den + jnp.exp(sink_ref[g] - m)
        outs.append(jnp.dot(e.astype(BF16), vb, preferred_element_type=F32) / den)
    _store_paired(o_ref, outs[0], outs[1], tq)


def _band_attn(q, kp, vp, bias, sink, window):
    n, s, _ = q.shape
    tq = Q_BLOCK
    band = window + tq
    use_sink = sink is not None
    in_specs = [pl.BlockSpec((1, tq, HD), lambda b, i: (b, i, 0)),
                pl.BlockSpec((1, window + s, KVD), lambda b, i: (b, 0, 0)),
                pl.BlockSpec((1, window + s, KVD), lambda b, i: (b, 0, 0)),
                pl.BlockSpec((N_KV_HEADS, GROUP * tq, band), lambda b, i: (0, 0, 0))]
    args = [q, kp, vp, bias]
    if use_sink:
        in_specs.append(pl.BlockSpec((N_KV_HEADS, GROUP * tq, 1), lambda b, i: (0, 0, 0)))
        args.append(sink)
    return pl.pallas_call(
        functools.partial(_band_attn_kernel, tq=tq, window=window, use_sink=use_sink),
        grid=(n, s // tq),
        in_specs=in_specs,
        out_specs=pl.BlockSpec((1, tq, HD), lambda b, i: (b, i, 0)),
        out_shape=jax.ShapeDtypeStruct((n, s, HD), BF16),
        scratch_shapes=[pltpu.VMEM((N_KV_HEADS, GROUP * tq, LANES), BF16)],
        compiler_params=_params(("parallel", "arbitrary")),
        name="band_attn",
    )(*args)


def _cmp_softmax(s3, negm, has_valid):
    s3 = s3 + negm
    m = jnp.max(s3, axis=-1, keepdims=True)
    m = jnp.where(has_valid, m, 0.0)
    e = jnp.exp(s3 - m)
    den = jnp.sum(e, axis=-1, keepdims=True)
    return e / jnp.where(den > 0, den, 1.0)


def _cmp_mask(t, nsp):
    u = lax.broadcasted_iota(jnp.int32, (t.shape[0], 2 * nsp), 1)
    cend = jnp.where(u < nsp, SEL_BLOCK * u + (CMP_BLOCK - 1), SEL_BLOCK * (u - nsp) + (SEL_BLOCK - 1))
    return jnp.where(cend <= t, 0.0, NEG), t >= CMP_BLOCK - 1


def _select_blocks(imp, t):
    rows, nsp = imp.shape
    j = lax.broadcasted_iota(jnp.int32, (rows, nsp), 1)
    jf = j.astype(F32)
    cur = t // SEL_BLOCK
    forced = (j == 0) | (j == cur) | (j == cur - 1)
    work = jnp.where(forced, jnp.inf, jnp.where(j <= cur, imp, -jnp.inf))
    sel = jnp.zeros((rows, nsp), F32)
    lane = lax.broadcasted_iota(jnp.int32, (rows, LANES), 1)
    idx = jnp.full((rows, LANES), -1.0, F32)
    for k in range(SEL_TOPK):
        m = jnp.max(work, axis=-1, keepdims=True)
        first = jnp.min(jnp.where(work == m, jf, float(nsp)), axis=-1, keepdims=True)
        hit = jf == first
        ok = m > -jnp.inf
        sel = jnp.where(hit & ok, 1.0, sel)
        idx = jnp.where(lane == k, jnp.where(ok, first, -1.0), idx)
        work = jnp.where(hit, -jnp.inf, work)
    return sel, idx


def _cmp_topk_kernel(q_ref, ck_ref, cv_ref, o_ref, sel_ref, qs_ref, *, tq, nsp):
    s0 = pl.program_id(1) * tq
    _stack_queries(q_ref, qs_ref, tq)
    t = s0 + lax.broadcasted_iota(jnp.int32, (tq, 1), 0)
    negm, has_valid = _cmp_mask(t, nsp)
    ck, cv = ck_ref[0], cv_ref[0]
    outs = []
    for g in range(N_KV_HEADS):
        s3 = _nt_dot(qs_ref[g], ck).reshape(GROUP, tq, 2 * nsp)
        p = _cmp_softmax(s3, negm[None], has_valid[None])
        outs.append(jnp.dot(p.reshape(GROUP * tq, 2 * nsp).astype(BF16), cv, preferred_element_type=F32))
        ph = jnp.sum(p, axis=0)
        sel, _ = _select_blocks(ph[:, :nsp] + ph[:, nsp:], t)
        sel_ref[0, g] = sel.astype(sel_ref.dtype)
    _store_paired(o_ref, outs[0], outs[1], tq)


def _cmp_topk(q, ck, cv):
    n, s, _ = q.shape
    tq = Q_BLOCK
    nsp = ck.shape[1] // 2
    return pl.pallas_call(
        functools.partial(_cmp_topk_kernel, tq=tq, nsp=nsp),
        grid=(n, s // tq),
        in_specs=[pl.BlockSpec((1, tq, HD), lambda b, i: (b, i, 0)),
                  pl.BlockSpec((1, 2 * nsp, KVD), lambda b, i: (b, 0, 0)),
                  pl.BlockSpec((1, 2 * nsp, KVD), lambda b, i: (b, 0, 0))],
        out_specs=[pl.BlockSpec((1, tq, HD), lambda b, i: (b, i, 0)),
                   pl.BlockSpec((1, N_KV_HEADS, tq, nsp), lambda b, i: (b, 0, i, 0))],
        out_shape=[jax.ShapeDtypeStruct((n, s, HD), BF16),
                   jax.ShapeDtypeStruct((n, N_KV_HEADS, s, nsp), BF16)],
        scratch_shapes=[pltpu.VMEM((N_KV_HEADS, GROUP * tq, LANES), BF16)],
        compiler_params=_params(("parallel", "arbitrary")),
        name="cmp_topk",
    )(q, ck, cv)


def _sel_attn_kernel(q_ref, sel_ref, k_ref, v_ref, et_ref, nb_ref, o_ref, qs_ref, m_ref, l_ref, acc_ref, *, tq):
    qt = pl.program_id(1)
    s0 = pl.multiple_of(qt * tq, tq)
    _stack_queries(q_ref, qs_ref, tq)

    def allow_bias(g, off, width):
        allow = _nt_dot(sel_ref[0, g], et_ref[pl.ds(off, width), :])
        return (allow - 1.0) * (-NEG)

    kn = k_ref[0, pl.ds(s0, 2 * tq), :]
    vn = v_ref[0, pl.ds(s0, 2 * tq), :]
    for g in range(N_KV_HEADS):
        s3 = _nt_dot(qs_ref[g], kn).reshape(GROUP, tq, 2 * tq)
        s3 = s3 + allow_bias(g, s0, 2 * tq)[None] + nb_ref[g].reshape(GROUP, tq, 2 * tq)
        m = jnp.max(s3, axis=-1, keepdims=True)
        e = jnp.exp(s3 - m)
        m_ref[g] = m.reshape(GROUP * tq, 1)
        l_ref[g] = jnp.sum(e, axis=-1, keepdims=True).reshape(GROUP * tq, 1)
        acc_ref[g] = jnp.dot(e.reshape(GROUP * tq, 2 * tq).astype(BF16), vn, preferred_element_type=F32)

    def far_tile(kt, carry):
        off = pl.multiple_of((kt + 1) * tq, tq)
        kf = k_ref[0, pl.ds(off, tq), :]
        vf = v_ref[0, pl.ds(off, tq), :]
        for g in range(N_KV_HEADS):
            s3 = _nt_dot(qs_ref[g], kf).reshape(GROUP, tq, tq) + allow_bias(g, off, tq)[None]
            s = s3.reshape(GROUP * tq, tq)
            m_old = m_ref[g]
            m_new = jnp.maximum(m_old, jnp.max(s, axis=-1, keepdims=True))
            alpha = jnp.exp(m_old - m_new)
            e = jnp.exp(s - m_new)
            m_ref[g] = m_new
            l_ref[g] = alpha * l_ref[g] + jnp.sum(e, axis=-1, keepdims=True)
            acc_ref[g] = alpha * acc_ref[g] + jnp.dot(e.astype(BF16), vf, preferred_element_type=F32)
        return carry

    lax.fori_loop(0, jnp.maximum(qt - 1, 0), far_tile, 0)
    _store_paired(o_ref, acc_ref[0] / l_ref[0], acc_ref[1] / l_ref[1], tq)


def _sel_attn(q, sel, kp, vp, et, nb):
    n, s, _ = q.shape
    tq = Q_BLOCK
    nsp = sel.shape[-1]
    return pl.pallas_call(
        functools.partial(_sel_attn_kernel, tq=tq),
        grid=(n, s // tq),
        in_specs=[pl.BlockSpec((1, tq, HD), lambda b, i: (b, i, 0)),
                  pl.BlockSpec((1, N_KV_HEADS, tq, nsp), lambda b, i: (b, 0, i, 0)),
                  pl.BlockSpec((1, tq + s, KVD), lambda b, i: (b, 0, 0)),
                  pl.BlockSpec((1, tq + s, KVD), lambda b, i: (b, 0, 0)),
                  pl.BlockSpec((tq + s, nsp), lambda b, i: (0, 0)),
                  pl.BlockSpec((N_KV_HEADS, GROUP * tq, 2 * tq), lambda b, i: (0, 0, 0))],
        out_specs=pl.BlockSpec((1, tq, HD), lambda b, i: (b, i, 0)),
        out_shape=jax.ShapeDtypeStruct((n, s, HD), BF16),
        scratch_shapes=[pltpu.VMEM((N_KV_HEADS, GROUP * tq, LANES), BF16),
                        pltpu.VMEM((N_KV_HEADS, GROUP * tq, 1), F32),
                        pltpu.VMEM((N_KV_HEADS, GROUP * tq, 1), F32),
                        pltpu.VMEM((N_KV_HEADS, GROUP * tq, LANES), F32)],
        compiler_params=_params(("parallel", "arbitrary")),
        name="sel_attn",
    )(q, sel, kp, vp, et, nb)


def _finish_kernel(*refs, n_branch, final):
    x_ref, gate_ref, z_ref = refs[:3]
    pos = 3
    if n_branch > 1:
        gl_ref, e_ref = refs[pos:pos + 2]
        pos += 2
    o_refs = refs[pos:pos + n_branch]
    pos += n_branch
    w_ref = refs[pos]
    pos += 1
    if final:
        fg_ref = refs[pos]
        pos += 1
    out_ref = refs[pos]
    mix = None
    if n_branch > 1:
        sg = jax.nn.sigmoid(gl_ref[0]).astype(BF16)
    for br in range(n_branch):
        t = o_refs[br][0].astype(F32) * _silu(z_ref[0, :, HD * br:HD * (br + 1)].astype(F32))
        if n_branch > 1:
            t = t * jnp.dot(sg, e_ref[:, HD * br:HD * (br + 1)], preferred_element_type=F32)
        mix = t if mix is None else mix + t
    upd = jnp.dot(mix.astype(BF16), w_ref[...], preferred_element_type=F32)
    xn = x_ref[0] + gate_ref[0] * upd
    out_ref[0] = xn
    if final:
        y = xn * lax.rsqrt(jnp.mean(xn * xn, axis=-1, keepdims=True) + RMS_EPS) * fg_ref[...]
        refs[pos + 1][0] = y


def _finish(x, gate, z, os_, w, gl=None, e=None, final_g=None, tm=512):
    nb, t, d = x.shape
    tm = min(tm, t)
    n_branch = len(os_)
    final = final_g is not None
    row = lambda c: pl.BlockSpec((1, tm, c), lambda n, i: (n, i, 0))
    if gate.shape[1] == 1:
        gate_spec = pl.BlockSpec((1, 1, d), lambda n, i: (n, 0, 0))
    else:
        gate_spec = row(d)
    in_specs = [row(d), gate_spec, row(z.shape[-1])]
    args = [x, gate, z]
    if n_branch > 1:
        in_specs += [row(gl.shape[-1]), pl.BlockSpec(e.shape, lambda n, i: (0, 0))]
        args += [gl, e]
    in_specs += [row(HD)] * n_branch + [pl.BlockSpec(w.shape, lambda n, i: (0, 0))]
    args += list(os_) + [w]
    out_specs, out_shape = [row(d)], [jax.ShapeDtypeStruct((nb, t, d), F32)]
    if final:
        in_specs.append(pl.BlockSpec((1, d), lambda n, i: (0, 0)))
        args.append(final_g.reshape(1, d))
        out_specs.append(row(d))
        out_shape.append(jax.ShapeDtypeStruct((nb, t, d), F32))
    res = pl.pallas_call(
        functools.partial(_finish_kernel, n_branch=n_branch, final=final),
        grid=(nb, t // tm),
        in_specs=in_specs, out_specs=out_specs, out_shape=out_shape,
        compiler_params=_params(("parallel", "parallel")),
        name="finish",
    )(*args)
    return res if final else res[0]


HALF_ROWS = 2 * CMP_BLOCK


def _compress_rows(xb, pos_ref, w1_ref, w2_ref):
    outs = []
    for k in range(2):
        hid = None
        for b in range(CMP_BLOCK):
            r = 2 * b + k
            xv = (xb(r) + pos_ref[r:r + 1, :]).astype(BF16)
            part = jnp.dot(xv, w1_ref[r], preferred_element_type=F32)
            hid = part if hid is None else hid + part
        outs.append(jnp.dot(_silu(hid).astype(BF16), w2_ref[k], preferred_element_type=F32))
    return jnp.concatenate(outs, axis=1)


def _compress_kernel(x_ref, pos_ref, w1_ref, w2_ref, o_ref, *, nblk):
    o_ref[0] = _compress_rows(lambda r: x_ref[0, pl.ds(r, nblk, stride=HALF_ROWS), :], pos_ref, w1_ref, w2_ref)


def _compress(rows, pos_e, w1e, w2e):
    nb, t, _ = rows.shape
    nblk = t // CMP_BLOCK
    return pl.pallas_call(
        functools.partial(_compress_kernel, nblk=nblk),
        grid=(nb,),
        in_specs=[pl.BlockSpec((1, 2 * t, KVD), lambda n: (n, 0, 0)),
                  pl.BlockSpec(pos_e.shape, lambda n: (0, 0)),
                  pl.BlockSpec(w1e.shape, lambda n: (0, 0, 0)),
                  pl.BlockSpec(w2e.shape, lambda n: (0, 0, 0))],
        out_specs=pl.BlockSpec((1, nblk, KV_ROW), lambda n: (n, 0, 0)),
        out_shape=jax.ShapeDtypeStruct((nb, nblk, KV_ROW), F32),
        compiler_params=_params(("parallel",)),
        name="compress",
    )(rows.reshape(nb, 2 * t, KVD), pos_e, w1e, w2e)


def _paged_copies(pt_ref, pool_ref, buf_ref, sem_ref, n, n_pages):
    return [pltpu.make_async_copy(pool_ref.at[pt_ref[n * n_pages + p]],
                                  buf_ref.at[pl.ds(p * 2 * PAGE_SIZE, 2 * PAGE_SIZE)], sem_ref.at[p])
            for p in range(n_pages)]


def _compress_paged_kernel(pt_ref, pool_ref, pos_ref, w1_ref, w2_ref, o_ref, buf_ref, sem_ref, *, n_pages):
    n = pl.program_id(0)
    copies = _paged_copies(pt_ref, pool_ref, buf_ref, sem_ref, n, n_pages)
    for cp in copies:
        cp.start()
    for cp in copies:
        cp.wait()
    nblk = n_pages * PAGE_SIZE // CMP_BLOCK
    o_ref[0] = _compress_rows(lambda r: buf_ref[pl.ds(r, nblk, stride=HALF_ROWS), :], pos_ref, w1_ref, w2_ref)


def _compress_paged(pool, page_table, pos_e, w1e, w2e):
    ns, n_pages = page_table.shape
    past = n_pages * PAGE_SIZE
    nblk = past // CMP_BLOCK
    grid_spec = pltpu.PrefetchScalarGridSpec(
        num_scalar_prefetch=1,
        grid=(ns,),
        in_specs=[pl.BlockSpec(memory_space=pl.ANY),
                  pl.BlockSpec(pos_e.shape, lambda n, pt: (0, 0)),
                  pl.BlockSpec(w1e.shape, lambda n, pt: (0, 0, 0)),
                  pl.BlockSpec(w2e.shape, lambda n, pt: (0, 0, 0))],
        out_specs=pl.BlockSpec((1, nblk, KV_ROW), lambda n, pt: (n, 0, 0)),
        scratch_shapes=[pltpu.VMEM((2 * past, KVD), F32), pltpu.SemaphoreType.DMA((n_pages,))])
    return pl.pallas_call(
        functools.partial(_compress_paged_kernel, n_pages=n_pages),
        grid_spec=grid_spec,
        out_shape=jax.ShapeDtypeStruct((ns, nblk, KV_ROW), F32),
        compiler_params=_params(("arbitrary",)),
        name="compress_paged",
    )(page_table.reshape(-1), pool.reshape(-1, 2 * PAGE_SIZE, KVD), pos_e, w1e, w2e)


def _decode_queries(q):
    lane = lax.broadcasted_iota(jnp.int32, q.shape, 1)
    qs = q * ATTN_SCALE
    zero = jnp.zeros_like(qs)
    return [jnp.concatenate([jnp.where(lane < HEAD_DIM, qs, 0.0), zero], axis=1),
            jnp.concatenate([jnp.where(lane >= HEAD_DIM, qs, 0.0), zero], axis=1)]


def _decode_group(qg, kv, bias, new_row, bias_new, new_on, sink):
    s = _nt_dot(qg.astype(BF16), kv) + bias
    s_new = jnp.sum(qg * new_row, axis=-1, keepdims=True) + bias_new
    if new_on is not None:
        s_new = jnp.where(new_on, s_new, NEG)
    m = jnp.maximum(jnp.max(s, axis=-1, keepdims=True), s_new)
    if sink is not None:
        m = jnp.maximum(m, sink)
    e = jnp.exp(s - m)
    e_new = jnp.exp(s_new - m)
    den = jnp.sum(e, axis=-1, keepdims=True) + e_new
    if sink is not None:
        den = den + jnp.exp(sink - m)
    o = jnp.dot(e.astype(BF16), kv, preferred_element_type=F32) + e_new * new_row
    return o / den


def _pair_values(o0, o1):
    lane = lax.broadcasted_iota(jnp.int32, (GROUP, LANES), 1)
    return jnp.where(lane < HEAD_DIM, o0[:, KVD:], o1[:, KVD:])


def _dec_attn_kernel(*refs, nb, use_sink):
    if use_sink:
        q_ref, kv_ref, new_ref, bias_ref, bnew_ref, sink_ref, o_ref = refs
    else:
        q_ref, kv_ref, new_ref, bias_ref, bnew_ref, o_ref = refs
    for b in range(nb):
        qgs = _decode_queries(q_ref[b].astype(F32))
        kv = kv_ref[b].astype(BF16)
        outs = [_decode_group(qgs[g], kv, bias_ref[g], new_ref[b], bnew_ref[g], None,
                              sink_ref[g] if use_sink else None) for g in range(N_KV_HEADS)]
        o_ref[b] = _pair_values(outs[0], outs[1]).astype(o_ref.dtype)


def _dec_attn(q, buf, new, bias, bias_new, sink):
    ns, l, _ = buf.shape
    nb = 8
    use_sink = sink is not None
    const = lambda a: pl.BlockSpec(a.shape, lambda i: (0,) * a.ndim)
    in_specs = [pl.BlockSpec((nb, GROUP, LANES), lambda i: (i, 0, 0)),
                pl.BlockSpec((nb, l, KV_ROW), lambda i: (i, 0, 0)),
                pl.BlockSpec((nb, 1, KV_ROW), lambda i: (i, 0, 0)),
                const(bias), const(bias_new)]
    args = [q, buf, new, bias, bias_new]
    if use_sink:
        in_specs.append(const(sink))
        args.append(sink)
    return pl.pallas_call(
        functools.partial(_dec_attn_kernel, nb=nb, use_sink=use_sink),
        grid=(ns // nb,),
        in_specs=in_specs,
        out_specs=pl.BlockSpec((nb, GROUP, LANES), lambda i: (i, 0, 0)),
        out_shape=jax.ShapeDtypeStruct((ns, GROUP, LANES), BF16),
        compiler_params=_params(("parallel",)),
        name="dec_attn",
    )(*args)


def _dec_cmp_topk_kernel(q_ref, ck_ref, cv_ref, o_ref, idx_ref, imp_ref, *, nb, nsp, qpos):
    t1 = jnp.full((1, 1), qpos, jnp.int32)
    negm, has_valid = _cmp_mask(t1, nsp)
    lane = lax.broadcasted_iota(jnp.int32, (GROUP, LANES), 1)
    for b in range(nb):
        q = q_ref[b].astype(F32) * ATTN_SCALE
        outs = []
        for g in range(N_KV_HEADS):
            qg = jnp.where((lane < HEAD_DIM) == (g == 0), q, 0.0).astype(BF16)
            p = _cmp_softmax(_nt_dot(qg, ck_ref[b]), negm, has_valid)
            outs.append(jnp.dot(p.astype(BF16), cv_ref[b], preferred_element_type=F32))
            ph = jnp.sum(p, axis=0, keepdims=True)
            imp_ref[N_KV_HEADS * b + g:N_KV_HEADS * b + g + 1, :] = ph[:, :nsp] + ph[:, nsp:]
        o_ref[b] = jnp.where(lane < HEAD_DIM, outs[0], outs[1]).astype(o_ref.dtype)
    rows = N_KV_HEADS * nb
    _, idx = _select_blocks(imp_ref[...], jnp.full((rows, 1), qpos, jnp.int32))
    idx_ref[0] = idx.astype(jnp.int32)


def _dec_cmp_topk(q, ck, cv, qpos):
    ns = q.shape[0]
    nb = 8
    nsp = ck.shape[1] // 2
    return pl.pallas_call(
        functools.partial(_dec_cmp_topk_kernel, nb=nb, nsp=nsp, qpos=qpos),
        grid=(ns // nb,),
        in_specs=[pl.BlockSpec((nb, GROUP, LANES), lambda i: (i, 0, 0)),
                  pl.BlockSpec((nb, 2 * nsp, KVD), lambda i: (i, 0, 0)),
                  pl.BlockSpec((nb, 2 * nsp, KVD), lambda i: (i, 0, 0))],
        out_specs=[pl.BlockSpec((nb, GROUP, LANES), lambda i: (i, 0, 0)),
                   pl.BlockSpec((1, N_KV_HEADS * nb, LANES), lambda i: (i, 0, 0))],
        out_shape=[jax.ShapeDtypeStruct((ns, GROUP, LANES), BF16),
                   jax.ShapeDtypeStruct((ns // nb, N_KV_HEADS * nb, LANES), jnp.int32)],
        scratch_shapes=[pltpu.VMEM((N_KV_HEADS * nb, nsp), F32)],
        compiler_params=_params(("parallel",)),
        name="dec_cmp_topk",
    )(q, ck, cv)


def _sel_copies(idx_ref, pt_ref, pool_ref, buf_ref, sem_ref, n, n_pages):
    half_pages = PAGE_SIZE // SEL_BLOCK
    n_past = n_pages * half_pages
    copies = []
    for g in range(N_KV_HEADS):
        for k in range(SEL_TOPK):
            blk = jnp.clip(idx_ref[(n * N_KV_HEADS + g) * SEL_TOPK + k], 0, n_past - 1)
            phys = pt_ref[n * n_pages + blk // half_pages]
            copies.append(pltpu.make_async_copy(
                pool_ref.at[phys * half_pages + blk % half_pages],
                buf_ref.at[g, pl.ds(k * SEL_BLOCK, SEL_BLOCK)], sem_ref.at[g, k]))
    return copies


def _dec_sel_kernel(idx_ref, pt_ref, q_ref, new_ref, bt_ref, bnew_ref, pool_ref, o_ref, buf_ref, bias_ref,
                    sem_ref, *, n_pages):
    n = pl.program_id(0)
    n_past = n_pages * (PAGE_SIZE // SEL_BLOCK)
    copies = _sel_copies(idx_ref, pt_ref, pool_ref, buf_ref, sem_ref, n, n_pages)
    for cp in copies:
        cp.start()
    lane = lax.broadcasted_iota(jnp.int32, (GROUP, LANES), 1)
    new_on = []
    for g in range(N_KV_HEADS):
        has_new = False
        for k2 in range(SEL_TOPK // 2):
            parts = []
            for k in (2 * k2, 2 * k2 + 1):
                blk = idx_ref[(n * N_KV_HEADS + g) * SEL_TOPK + k]
                cached = (blk >= 0) & (blk < n_past)
                row = bt_ref[jnp.clip(blk, 0, n_past - 1), GROUP * g:GROUP * (g + 1), :]
                parts.append(row + jnp.where(cached, 0.0, NEG))
                has_new = has_new | (blk == n_past)
            bias_ref[g, :, LANES * k2:LANES * (k2 + 1)] = jnp.where(lane < SEL_BLOCK, parts[0], parts[1])
        new_on.append(has_new)
    for cp in copies:
        cp.wait()
    qgs = _decode_queries(q_ref[0].astype(F32))
    outs = [_decode_group(qgs[g], buf_ref[g].astype(BF16), bias_ref[g], new_ref[0], bnew_ref[g], new_on[g], None)
            for g in range(N_KV_HEADS)]
    o_ref[0] = _pair_values(outs[0], outs[1]).astype(o_ref.dtype)


def _dec_sel(idx, page_table, q, new, bt, bias_new, pool):
    ns, n_pages = page_table.shape
    nkeys = SEL_TOPK * SEL_BLOCK
    grid_spec = pltpu.PrefetchScalarGridSpec(
        num_scalar_prefetch=2,
        grid=(ns,),
        in_specs=[pl.BlockSpec((1, GROUP, LANES), lambda n, ix, pt: (n, 0, 0)),
                  pl.BlockSpec((1, 1, KV_ROW), lambda n, ix, pt: (n, 0, 0)),
                  pl.BlockSpec(bt.shape, lambda n, ix, pt: (0, 0, 0)),
                  pl.BlockSpec(bias_new.shape, lambda n, ix, pt: (0, 0, 0)),
                  pl.BlockSpec(memory_space=pl.ANY)],
        out_specs=pl.BlockSpec((1, GROUP, LANES), lambda n, ix, pt: (n, 0, 0)),
        scratch_shapes=[pltpu.VMEM((N_KV_HEADS, nkeys, KV_ROW), F32),
                        pltpu.VMEM((N_KV_HEADS, GROUP, nkeys), F32),
                        pltpu.SemaphoreType.DMA((N_KV_HEADS, SEL_TOPK))])
    return pl.pallas_call(
        functools.partial(_dec_sel_kernel, n_pages=n_pages),
        grid_spec=grid_spec,
        out_shape=jax.ShapeDtypeStruct((ns, GROUP, LANES), BF16),
        compiler_params=_params(("arbitrary",)),
        name="dec_sel",
    )(idx, page_table.reshape(-1), q, new, bt, bias_new, pool)


def _t5_bucket(dist):
    n = jnp.maximum(dist, 0)
    exact = N_BUCKETS // 2
    nf = jnp.maximum(n, exact).astype(F32)
    large = exact + (jnp.log(nf / exact) / math.log(T5_MAX_DISTANCE / exact) * (N_BUCKETS - exact)).astype(jnp.int32)
    return jnp.where(n < exact, n, jnp.minimum(large, N_BUCKETS - 1))


def _stack_heads(b):
    return b.reshape(N_KV_HEADS, GROUP * b.shape[1], b.shape[2])


def _band_bias(table, tq, window):
    dist = jnp.arange(tq)[:, None] - jnp.arange(window + tq)[None, :] + window
    b = jnp.transpose(table.astype(F32)[_t5_bucket(dist)], (2, 0, 1))
    return _stack_heads(jnp.where(((dist >= 0) & (dist <= window))[None], b, NEG))


def _near_bias(table, tq):
    assert tq + 1 >= T5_MAX_DISTANCE
    dist = jnp.arange(tq)[:, None] - jnp.arange(2 * tq)[None, :] + tq
    tab = table.astype(F32)
    b = jnp.transpose(tab[_t5_bucket(dist)] - tab[N_BUCKETS - 1], (2, 0, 1))
    return _stack_heads(jnp.where((dist >= 0)[None], b, NEG))


def _decode_bias(table, length, window):
    dist = length - jnp.arange(length)
    tab = table.astype(F32)
    b = jnp.where((dist <= window)[None], tab[_t5_bucket(dist)].T, NEG)
    return b.reshape(N_KV_HEADS, GROUP, length), tab[0].reshape(N_KV_HEADS, GROUP, 1)


def _stack_rows(v, tq):
    return jnp.repeat(v.astype(F32).reshape(N_KV_HEADS, GROUP, 1), tq, axis=1).reshape(N_KV_HEADS, GROUP * tq, 1)


def _even_odd(comp, nsp):
    def half(x):
        return jnp.pad(x, ((0, 0), (0, nsp - x.shape[1]), (0, 0)))
    eo = jnp.concatenate([half(comp[:, 0::2]), half(comp[:, 1::2])], axis=1).astype(BF16)
    return eo[:, :, :KVD], eo[:, :, KVD:]


def _front_pad(x, rows):
    return jnp.pad(x, ((0, 0), (rows, 0), (0, 0)))


def kernel(x_prompt, x_sample, state_a_kv, cache_cmp_kv, cache_sel_kv, state_win_kv, page_table, c_prompt, c_sample, rel_bias_table, a_norm_g, a_w_ada, a_b_ada, a_w_in, a_sinks, a_w_out, kv_norm_g, kv_w_ada, kv_b_ada, kv_w_in, cmp_pos, cmp_w1, cmp_w2, b_norm_g, b_w_ada, b_b_ada, b_w_in, b_w_out, final_norm_g):
    n_p, seq, d = x_prompt.shape
    n_s = x_sample.shape[0]
    n_pages = page_table.shape[1]
    past = n_pages * PAGE_SIZE
    n_a, n_b = a_w_in.shape[0], b_w_in.shape[0]
    tq = Q_BLOCK
    assert x_sample.shape[1] == 1 and seq % 512 == 0 and n_s % 8 == 0
    assert seq // SEL_BLOCK >= SEL_TOPK and past // SEL_BLOCK >= SEL_TOPK
    assert state_a_kv.shape[2] == WINDOW_A and state_win_kv.shape[1] == WINDOW_B

    c_all = jnp.concatenate([c_prompt, jnp.zeros((8 - n_p, d), F32), c_sample], axis=0)
    ada_a = _ada(c_all, a_w_ada, a_b_ada)
    ada_b = _ada(c_all, b_w_ada, b_b_ada)
    ada_kv = _ada(c_all, kv_w_ada[None], kv_b_ada[None])

    def mods(ada, parts):
        pr = [ada[:n_p, None, i * d:(i + 1) * d] for i in range(parts)]
        sm = [ada[None, 8:8 + n_s, i * d:(i + 1) * d] for i in range(parts)]
        return pr, sm

    table = rel_bias_table
    xs = x_sample.reshape(1, n_s, d)
    xp = x_prompt

    bias_a = _band_bias(table, tq, WINDOW_A)
    dbias_a, dnew = _decode_bias(table, WINDOW_A, WINDOW_A)
    a_new_p, a_new_s = [], []
    for i in range(n_a):
        w_in = a_w_in[i]
        ws = [w_in[:, :HD][:, _PERM].astype(BF16), w_in[:, HD:HD + KV_ROW].astype(BF16),
              w_in[:, HD + KV_ROW:][:, _PERM].astype(BF16)]
        outs = [(0, 0, HD, BF16), (1, 0, KV_ROW, F32), (1, 0, KVD, BF16), (1, KVD, KV_ROW, BF16), (2, 0, HD, BF16)]
        w_out = a_w_out[i][_PERM, :].astype(BF16)
        (sh_p, sc_p, gt_p), (sh_s, sc_s, gt_s) = mods(ada_a[i], 3)
        q, kv, kb, vb, z = _mod_proj(xp, a_norm_g[i], sh_p, sc_p, ws, outs, 512)
        o = _band_attn(q, _front_pad(kb, WINDOW_A), _front_pad(vb, WINDOW_A), bias_a, _stack_rows(a_sinks[i], tq),
                       WINDOW_A)
        xp = _finish(xp, gt_p, z, [o], w_out)
        a_new_p.append(kv[:, seq - WINDOW_A:].reshape(n_p, WINDOW_A, 2, N_KV_HEADS, HEAD_DIM))
        q, kv, _, _, z = _mod_proj(xs, a_norm_g[i], sh_s, sc_s, ws, outs, 128)
        buf = state_a_kv[i].reshape(n_s, WINDOW_A, KV_ROW)
        new = kv.reshape(n_s, 1, KV_ROW)
        o = _dec_attn(q.reshape(n_s, GROUP, LANES), buf, new, dbias_a, dnew,
                      a_sinks[i].astype(F32).reshape(N_KV_HEADS, GROUP, 1))
        xs = _finish(xs, gt_s, z, [o.reshape(1, n_s, HD)], w_out)
        a_new_s.append(jnp.concatenate([buf[:, 1:], new], axis=1).reshape(n_s, WINDOW_A, 2, N_KV_HEADS, HEAD_DIM))

    (sh_p, sc_p), (sh_s, sc_s) = mods(ada_kv[0], 2)
    w_kv = [kv_w_in.astype(BF16)]
    outs = [(0, KV_ROW * b, KV_ROW * (b + 1), F32) for b in range(N_BRANCH)]
    outs += [(0, KV_ROW + KVD * h, KV_ROW + KVD * (h + 1), BF16) for h in range(4)]
    cmp_p, sel_p, win_p, sel_kp, sel_vp, win_kp, win_vp = _mod_proj(xp, kv_norm_g, sh_p, sc_p, w_kv, outs, 512)
    cmp_s, sel_s, win_s = _mod_proj(xs, kv_norm_g, sh_s, sc_s, w_kv, outs[:N_BRANCH], 128)
    cmp_s, sel_s, win_s = (r.reshape(n_s, 1, KV_ROW) for r in (cmp_s, sel_s, win_s))

    eye_g = jnp.eye(N_KV_HEADS, dtype=F32)
    w1e = jnp.einsum('kbdh,gG->bkgdGh', cmp_w1, eye_g).reshape(
        HALF_ROWS, KVD, N_KV_HEADS * CMP_HIDDEN).astype(BF16)
    w2e = jnp.einsum('khd,gG->kghGd', cmp_w2, eye_g).reshape(2, N_KV_HEADS * CMP_HIDDEN, KVD).astype(BF16)
    pos_e = jnp.broadcast_to(cmp_pos[:, :, None, :], (CMP_BLOCK, 2, N_KV_HEADS, HEAD_DIM)).reshape(HALF_ROWS, KVD)
    comp_p = _compress(cmp_p, pos_e, w1e, w2e)
    comp_past = _compress_paged(cache_cmp_kv.reshape(-1, PAGE_SIZE, KV_ROW), page_table, pos_e, w1e, w2e)
    tail = jnp.pad(cmp_s, ((0, 0), (0, SEL_BLOCK - 1), (0, 0))).reshape(1, n_s * SEL_BLOCK, KV_ROW)
    comp_tail = _compress(tail, pos_e, w1e, w2e).reshape(n_s, SEL_BLOCK // CMP_BLOCK, KV_ROW)
    comp_s = jnp.concatenate([comp_past, comp_tail], axis=1)

    nsp_p = -(-(seq // SEL_BLOCK) // LANES) * LANES
    nsp_s = -(-(past // SEL_BLOCK + 1) // LANES) * LANES
    ck_p, cv_p = _even_odd(comp_p, nsp_p)
    ck_s, cv_s = _even_odd(comp_s, nsp_s)

    sel_kp, sel_vp = _front_pad(sel_kp, tq), _front_pad(sel_vp, tq)
    win_kp, win_vp = _front_pad(win_kp, WINDOW_B), _front_pad(win_vp, WINDOW_B)
    key_block = (jnp.arange(tq + seq) - tq) // SEL_BLOCK
    et = ((key_block[:, None] == jnp.arange(nsp_p)[None, :]) & (jnp.arange(tq + seq) >= tq)[:, None]).astype(BF16)
    nb_sel = _near_bias(table, tq)
    bias_b = _band_bias(table, tq, WINDOW_B)
    dbias_b, _ = _decode_bias(table, WINDOW_B, WINDOW_B)
    win_buf = state_win_kv.reshape(n_s, WINDOW_B, KV_ROW)
    n_past = past // SEL_BLOCK
    kpos = jnp.arange(n_past * SEL_BLOCK).reshape(n_past, SEL_BLOCK)
    bt = jnp.transpose(table.astype(F32)[_t5_bucket(past - kpos)], (0, 2, 1))
    bt = jnp.concatenate([bt, bt], axis=-1)
    sel_pool = cache_sel_kv.reshape(-1, SEL_BLOCK, KV_ROW)

    head_of = _PERM // HEAD_DIM
    e_np = np.zeros((LANES, N_BRANCH * HD), np.float32)
    for br in range(N_BRANCH):
        e_np[br * N_HEADS + head_of, br * HD + np.arange(HD)] = 1.0
    e_gl = jnp.asarray(e_np, dtype=BF16)
    zperm = np.concatenate([br * HD + _PERM for br in range(N_BRANCH)])

    y_p = y_s = None
    for j in range(n_b):
        w_in = b_w_in[j]
        ws = [w_in[:, :HD][:, _PERM].astype(BF16), w_in[:, HD:HD + N_BRANCH * HD][:, zperm].astype(BF16),
              jnp.pad(w_in[:, HD + N_BRANCH * HD:], ((0, 0), (0, LANES - N_BRANCH * N_HEADS))).astype(BF16)]
        outs = [(0, 0, HD, BF16), (1, 0, N_BRANCH * HD, BF16), (2, 0, LANES, F32)]
        w_out = b_w_out[j][_PERM, :].astype(BF16)
        fg = final_norm_g if j == n_b - 1 else None
        (sh_p, sc_p, gt_p), (sh_s, sc_s, gt_s) = mods(ada_b[j], 3)
        q, z, gl = _mod_proj(xp, b_norm_g[j], sh_p, sc_p, ws, outs, 256)
        o_cmp, sel = _cmp_topk(q, ck_p, cv_p)
        o_sel = _sel_attn(q, sel, sel_kp, sel_vp, et, nb_sel)
        o_win = _band_attn(q, win_kp, win_vp, bias_b, None, WINDOW_B)
        r = _finish(xp, gt_p, z, [o_cmp, o_sel, o_win], w_out, gl, e_gl, fg, 256)
        xp, y_p = r if fg is not None else (r, None)
        q, z, gl = _mod_proj(xs, b_norm_g[j], sh_s, sc_s, ws, outs, 128)
        q = q.reshape(n_s, GROUP, LANES)
        o_cmp, idx = _dec_cmp_topk(q, ck_s, cv_s, past)
        idx = idx.reshape(n_s, N_KV_HEADS, LANES)[:, :, :SEL_TOPK].reshape(-1)
        o_sel = _dec_sel(idx, page_table, q, sel_s, bt, dnew, sel_pool)
        o_win = _dec_attn(q, win_buf, win_s, dbias_b, dnew, None)
        r = _finish(xs, gt_s, z, [o.reshape(1, n_s, HD) for o in (o_cmp, o_sel, o_win)], w_out, gl, e_gl, fg, 128)
        xs, y_s = r if fg is not None else (r, None)

    kv5 = lambda r: r.reshape(r.shape[0], r.shape[1], 2, N_KV_HEADS, HEAD_DIM)
    new_win_s = jnp.concatenate([win_buf[:, 1:], win_s], axis=1)
    return (y_p, y_s.reshape(n_s, 1, d), jnp.stack(a_new_p), jnp.stack(a_new_s), kv5(cmp_p), kv5(cmp_s),
            kv5(sel_p), kv5(sel_s), kv5(win_p[:, seq - WINDOW_B:]), kv5(new_win_s))
```

```python
import functools
import math

import numpy as np
import jax
import jax.numpy as jnp
from jax import lax
from jax.experimental import pallas as pl
from jax.experimental.pallas import tpu as pltpu

F32, BF16 = jnp.float32, jnp.bfloat16

D_MODEL = 1024
N_HEADS = 16
HEAD_DIM = 64
N_KV_HEADS = 2
GROUP = N_HEADS // N_KV_HEADS
HD = N_HEADS * HEAD_DIM
KVD = N_KV_HEADS * HEAD_DIM
KV_ROW = 2 * KVD
WINDOW_A = 128
WINDOW_B = 512
Q_BLOCK = 128
CMP_BLOCK = 32
SEL_BLOCK = 64
SEL_TOPK = 16
CMP_HIDDEN = 2 * HEAD_DIM
N_BRANCH = 3
N_BUCKETS = 32
T5_MAX_DISTANCE = 128
RMS_EPS = 1e-6
ATTN_SCALE = HEAD_DIM ** -0.5
PAGE_SIZE = 128
LANES = 128
NEG = -1e30
VMEM_LIMIT = 56 * 1024 * 1024

_PERM = np.array([(GROUP * g + c) * HEAD_DIM + d for c in range(GROUP) for g in range(N_KV_HEADS)
                  for d in range(HEAD_DIM)], dtype=np.int32)


def _params(sem):
    return pltpu.CompilerParams(dimension_semantics=sem, vmem_limit_bytes=VMEM_LIMIT)


def _silu(x):
    return x * jax.nn.sigmoid(x)


def _ada_kernel(c_ref, w_ref, b_ref, o_ref):
    a = _silu(c_ref[...]).astype(BF16)
    o_ref[0] = jnp.dot(a, w_ref[0].astype(BF16), preferred_element_type=F32) + b_ref[0]


def _ada(c_all, w, b):
    nl, d, n = w.shape
    m = c_all.shape[0]
    tn = 512
    return pl.pallas_call(
        _ada_kernel,
        grid=(nl, n // tn),
        in_specs=[pl.BlockSpec((m, d), lambda l, j: (0, 0)),
                  pl.BlockSpec((1, d, tn), lambda l, j: (l, 0, j)),
                  pl.BlockSpec((1, 1, tn), lambda l, j: (l, 0, j))],
        out_specs=pl.BlockSpec((1, m, tn), lambda l, j: (l, 0, j)),
        out_shape=jax.ShapeDtypeStruct((nl, m, n), F32),
        compiler_params=_params(("parallel", "parallel")),
        name="ada_params",
    )(c_all, w, b.reshape(nl, 1, n))


def _mod_proj_kernel(x_ref, g_ref, sh_ref, sc_ref, *refs, nw, outs):
    w_refs, o_refs = refs[:nw], refs[nw:]
    x = x_ref[0]
    y = x * lax.rsqrt(jnp.mean(x * x, axis=-1, keepdims=True) + RMS_EPS) * g_ref[...]
    h = (y * (1.0 + sc_ref[0]) + sh_ref[0]).astype(BF16)
    res = {}
    for (wi, c0, c1, dt), o_ref in zip(outs, o_refs):
        if wi not in res:
            res[wi] = jnp.dot(h, w_refs[wi][...], preferred_element_type=F32)
        o_ref[0] = res[wi][:, c0:c1].astype(dt)


def _mod_proj(x, g, shift, scale, ws, outs, tm):
    nb, t, d = x.shape
    tm = min(tm, t)
    if shift.shape[1] == 1:
        mod_spec = pl.BlockSpec((1, 1, d), lambda n, i: (n, 0, 0))
    else:
        mod_spec = pl.BlockSpec((1, tm, d), lambda n, i: (n, i, 0))
    in_specs = [pl.BlockSpec((1, tm, d), lambda n, i: (n, i, 0)),
                pl.BlockSpec((1, d), lambda n, i: (0, 0)), mod_spec, mod_spec]
    in_specs += [pl.BlockSpec(w.shape, lambda n, i: (0, 0)) for w in ws]
    out_specs = [pl.BlockSpec((1, tm, c1 - c0), lambda n, i: (n, i, 0)) for (_, c0, c1, _) in outs]
    out_shape = [jax.ShapeDtypeStruct((nb, t, c1 - c0), dt) for (_, c0, c1, dt) in outs]
    return pl.pallas_call(
        functools.partial(_mod_proj_kernel, nw=len(ws), outs=tuple(outs)),
        grid=(nb, t // tm),
        in_specs=in_specs, out_specs=out_specs, out_shape=out_shape,
        compiler_params=_params(("parallel", "parallel")),
        name="mod_proj",
    )(x, g.reshape(1, d), shift, scale, *ws)


def _stack_queries(q_ref, qs_ref, tq):
    lane = lax.broadcasted_iota(jnp.int32, (tq, LANES), 1)
    for c in range(GROUP):
        qc = (q_ref[0, :, LANES * c:LANES * (c + 1)].astype(F32) * ATTN_SCALE)
        qs_ref[0, c * tq:(c + 1) * tq, :] = jnp.where(lane < HEAD_DIM, qc, 0.0).astype(BF16)
        qs_ref[1, c * tq:(c + 1) * tq, :] = jnp.where(lane >= HEAD_DIM, qc, 0.0).astype(BF16)


def _store_paired(o_ref, o0, o1, tq):
    lane = lax.broadcasted_iota(jnp.int32, o0.shape, 1)
    oc = jnp.where(lane < HEAD_DIM, o0, o1)
    for c in range(GROUP):
        o_ref[0, :, LANES * c:LANES * (c + 1)] = oc[c * tq:(c + 1) * tq].astype(o_ref.dtype)


def _nt_dot(a, b):
    return lax.dot_general(a, b, (((1,), (1,)), ((), ())), preferred_element_type=F32)


def _row_max(s):
    cm = s[:, :LANES]
    for c in range(1, s.shape[1] // LANES):
        cm = jnp.maximum(cm, s[:, LANES * c:LANES * (c + 1)])
    return jnp.max(cm, axis=-1, keepdims=True)


def _lanes(m, width):
    return m if width == LANES else jnp.concatenate([m] * (width // LANES), axis=1)


def _normalize(acc):
    return acc / pltpu.roll(acc, HEAD_DIM, axis=1)


def _with_ones(v):
    lane = lax.broadcasted_iota(jnp.int32, v.shape, v.ndim - 1)
    one = jnp.ones_like(v)
    return jnp.where(lane < HEAD_DIM, v, one), jnp.where(lane >= HEAD_DIM, v, one)


def _band_attn_kernel(*refs, tq, window, use_sink):
    if use_sink:
        q_ref, k_ref, v0_ref, v1_ref, bias_ref, sink_ref, o_ref, qs_ref = refs
    else:
        q_ref, k_ref, v0_ref, v1_ref, bias_ref, o_ref, qs_ref = refs
    s0 = pl.multiple_of(pl.program_id(1) * tq, tq)
    band = window + tq
    rows = GROUP * tq
    _stack_queries(q_ref, qs_ref, tq)
    kb = k_ref[0, pl.ds(s0, band), :]
    kpos = s0 - window + lax.broadcasted_iota(jnp.int32, (1, band), 1)
    front = jnp.where(kpos >= 0, 0.0, NEG)
    lane = lax.broadcasted_iota(jnp.int32, (rows, LANES), 1)
    outs = []
    for g, v_ref in enumerate((v0_ref, v1_ref)):
        s = _nt_dot(qs_ref[g], kb) + bias_ref[g] + front
        m = jnp.broadcast_to(_row_max(s), (rows, LANES))
        if use_sink:
            m = jnp.maximum(m, sink_ref[g])
        e = jnp.exp(s - _lanes(m, band))
        acc = jnp.dot(e.astype(BF16), v_ref[0, pl.ds(s0, band), :], preferred_element_type=F32)
        if use_sink:
            acc = acc + jnp.where((lane < HEAD_DIM) == (g == 0), 0.0, jnp.exp(sink_ref[g] - m))
        outs.append(_normalize(acc))
    _store_paired(o_ref, outs[0], outs[1], tq)


def _band_attn(q, kp, vp0, vp1, bias, sink, window):
    n, s, _ = q.shape
    tq = Q_BLOCK
    band = window + tq
    use_sink = sink is not None
    kv_spec = pl.BlockSpec((1, window + s, KVD), lambda b, i: (b, 0, 0))
    in_specs = [pl.BlockSpec((1, tq, HD), lambda b, i: (b, i, 0)), kv_spec, kv_spec, kv_spec,
                pl.BlockSpec((N_KV_HEADS, GROUP * tq, band), lambda b, i: (0, 0, 0))]
    args = [q, kp, vp0, vp1, bias]
    if use_sink:
        in_specs.append(pl.BlockSpec((N_KV_HEADS, GROUP * tq, LANES), lambda b, i: (0, 0, 0)))
        args.append(sink)
    return pl.pallas_call(
        functools.partial(_band_attn_kernel, tq=tq, window=window, use_sink=use_sink),
        grid=(n, s // tq),
        in_specs=in_specs,
        out_specs=pl.BlockSpec((1, tq, HD), lambda b, i: (b, i, 0)),
        out_shape=jax.ShapeDtypeStruct((n, s, HD), BF16),
        scratch_shapes=[pltpu.VMEM((N_KV_HEADS, GROUP * tq, LANES), BF16)],
        compiler_params=_params(("parallel", "arbitrary")),
        name="band_attn",
    )(*args)


def _cmp_softmax(s3, negm, has_valid):
    s3 = s3 + negm
    m = jnp.max(s3, axis=-1, keepdims=True)
    m = jnp.where(has_valid, m, 0.0)
    e = jnp.exp(s3 - m)
    den = jnp.sum(e, axis=-1, keepdims=True)
    return e / jnp.where(den > 0, den, 1.0)


def _cmp_mask(t, nsp):
    u = lax.broadcasted_iota(jnp.int32, (t.shape[0], 2 * nsp), 1)
    cend = jnp.where(u < nsp, SEL_BLOCK * u + (CMP_BLOCK - 1), SEL_BLOCK * (u - nsp) + (SEL_BLOCK - 1))
    return jnp.where(cend <= t, 0.0, NEG), t >= CMP_BLOCK - 1


def _select_blocks(imp, t):
    rows, nsp = imp.shape
    j = lax.broadcasted_iota(jnp.int32, (rows, nsp), 1)
    jf = j.astype(F32)
    cur = t // SEL_BLOCK
    forced = (j == 0) | (j == cur) | (j == cur - 1)
    work = jnp.where(forced, jnp.inf, jnp.where(j <= cur, imp, -jnp.inf))
    sel = jnp.zeros((rows, nsp), F32)
    lane = lax.broadcasted_iota(jnp.int32, (rows, LANES), 1)
    idx = jnp.full((rows, LANES), -1.0, F32)
    for k in range(SEL_TOPK):
        m = jnp.max(work, axis=-1, keepdims=True)
        first = jnp.min(jnp.where(work == m, jf, float(nsp)), axis=-1, keepdims=True)
        hit = jf == first
        ok = m > -jnp.inf
        sel = jnp.where(hit & ok, 1.0, sel)
        idx = jnp.where(lane == k, jnp.where(ok, first, -1.0), idx)
        work = jnp.where(hit, -jnp.inf, work)
    return sel, idx


def _cmp_topk_kernel(q_ref, ck_ref, cv_ref, o_ref, sel_ref, qs_ref, *, tq, nsp):
    s0 = pl.program_id(1) * tq
    _stack_queries(q_ref, qs_ref, tq)
    t = s0 + lax.broadcasted_iota(jnp.int32, (tq, 1), 0)
    negm, has_valid = _cmp_mask(t, nsp)
    ck, cv = ck_ref[0], cv_ref[0]
    outs = []
    for g in range(N_KV_HEADS):
        s3 = _nt_dot(qs_ref[g], ck).reshape(GROUP, tq, 2 * nsp)
        p = _cmp_softmax(s3, negm[None], has_valid[None])
        outs.append(jnp.dot(p.reshape(GROUP * tq, 2 * nsp).astype(BF16), cv, preferred_element_type=F32))
        ph = jnp.sum(p, axis=0)
        sel, _ = _select_blocks(ph[:, :nsp] + ph[:, nsp:], t)
        sel_ref[0, g] = sel.astype(sel_ref.dtype)
    _store_paired(o_ref, outs[0], outs[1], tq)


def _cmp_topk(q, ck, cv):
    n, s, _ = q.shape
    tq = Q_BLOCK
    nsp = ck.shape[1] // 2
    return pl.pallas_call(
        functools.partial(_cmp_topk_kernel, tq=tq, nsp=nsp),
        grid=(n, s // tq),
        in_specs=[pl.BlockSpec((1, tq, HD), lambda b, i: (b, i, 0)),
                  pl.BlockSpec((1, 2 * nsp, KVD), lambda b, i: (b, 0, 0)),
                  pl.BlockSpec((1, 2 * nsp, KVD), lambda b, i: (b, 0, 0))],
        out_specs=[pl.BlockSpec((1, tq, HD), lambda b, i: (b, i, 0)),
                   pl.BlockSpec((1, N_KV_HEADS, tq, nsp), lambda b, i: (b, 0, i, 0))],
        out_shape=[jax.ShapeDtypeStruct((n, s, HD), BF16),
                   jax.ShapeDtypeStruct((n, N_KV_HEADS, s, nsp), BF16)],
        scratch_shapes=[pltpu.VMEM((N_KV_HEADS, GROUP * tq, LANES), BF16)],
        compiler_params=_params(("parallel", "arbitrary")),
        name="cmp_topk",
    )(q, ck, cv)


FAR_TILES = 4


def _sel_attn_kernel(q_ref, sel_ref, k_ref, v0_ref, v1_ref, et_ref, nb_ref, o_ref, qs_ref, m_ref, acc_ref, *, tq):
    qt = pl.program_id(1)
    s0 = pl.multiple_of(qt * tq, tq)
    rows = GROUP * tq
    v_refs = (v0_ref, v1_ref)
    _stack_queries(q_ref, qs_ref, tq)

    def key_tile(g, off, width, bias, first):
        kf = k_ref[0, pl.ds(off, width), :]
        vf = v_refs[g][0, pl.ds(off, width), :]
        allow = _nt_dot(sel_ref[0, g], et_ref[pl.ds(off, width), :])
        s3 = _nt_dot(qs_ref[g], kf).reshape(GROUP, tq, width) + ((allow - 1.0) * (-NEG))[None]
        if bias is not None:
            s3 = s3 + bias.reshape(GROUP, tq, width)
        s = s3.reshape(rows, width)
        rm = _row_max(s)
        if first:
            m_new = jnp.broadcast_to(rm, (rows, LANES))
        else:
            m_old = m_ref[g]
            m_new = jnp.maximum(m_old, rm)
        pv = jnp.dot(jnp.exp(s - _lanes(m_new, width)).astype(BF16), vf, preferred_element_type=F32)
        acc_ref[g] = pv if first else jnp.exp(m_old - m_new) * acc_ref[g] + pv
        m_ref[g] = m_new

    for g in range(N_KV_HEADS):
        key_tile(g, s0, 2 * tq, nb_ref[g], True)

    n_far = jnp.maximum(qt - 1, 0)
    n_wide = n_far // FAR_TILES

    def wide(j, carry):
        off = pl.multiple_of(tq + j * (FAR_TILES * tq), tq)
        for g in range(N_KV_HEADS):
            key_tile(g, off, FAR_TILES * tq, None, False)
        return carry

    def narrow(i, carry):
        off = pl.multiple_of(tq + (n_wide * FAR_TILES + i) * tq, tq)
        for g in range(N_KV_HEADS):
            key_tile(g, off, tq, None, False)
        return carry

    lax.fori_loop(0, n_wide, wide, 0)
    lax.fori_loop(0, n_far - n_wide * FAR_TILES, narrow, 0)
    _store_paired(o_ref, _normalize(acc_ref[0]), _normalize(acc_ref[1]), tq)


def _sel_attn(q, sel, kp, vp0, vp1, et, nb):
    n, s, _ = q.shape
    tq = Q_BLOCK
    nsp = sel.shape[-1]
    kv_spec = pl.BlockSpec((1, tq + s, KVD), lambda b, i: (b, 0, 0))
    return pl.pallas_call(
        functools.partial(_sel_attn_kernel, tq=tq),
        grid=(n, s // tq),
        in_specs=[pl.BlockSpec((1, tq, HD), lambda b, i: (b, i, 0)),
                  pl.BlockSpec((1, N_KV_HEADS, tq, nsp), lambda b, i: (b, 0, i, 0)),
                  kv_spec, kv_spec, kv_spec,
                  pl.BlockSpec((tq + s, nsp), lambda b, i: (0, 0)),
                  pl.BlockSpec((N_KV_HEADS, GROUP * tq, 2 * tq), lambda b, i: (0, 0, 0))],
        out_specs=pl.BlockSpec((1, tq, HD), lambda b, i: (b, i, 0)),
        out_shape=jax.ShapeDtypeStruct((n, s, HD), BF16),
        scratch_shapes=[pltpu.VMEM((N_KV_HEADS, GROUP * tq, LANES), BF16),
                        pltpu.VMEM((N_KV_HEADS, GROUP * tq, LANES), F32),
                        pltpu.VMEM((N_KV_HEADS, GROUP * tq, LANES), F32)],
        compiler_params=_params(("parallel", "arbitrary")),
        name="sel_attn",
    )(q, sel, kp, vp0, vp1, et, nb)


def _finish_kernel(*refs, n_branch, final):
    x_ref, gate_ref, z_ref = refs[:3]
    pos = 3
    if n_branch > 1:
        gl_ref, e_ref = refs[pos:pos + 2]
        pos += 2
    o_refs = refs[pos:pos + n_branch]
    pos += n_branch
    w_ref = refs[pos]
    pos += 1
    if final:
        fg_ref = refs[pos]
        pos += 1
    out_ref = refs[pos]
    mix = None
    if n_branch > 1:
        sg = jax.nn.sigmoid(gl_ref[0]).astype(BF16)
    for br in range(n_branch):
        t = o_refs[br][0].astype(F32) * _silu(z_ref[0, :, HD * br:HD * (br + 1)].astype(F32))
        if n_branch > 1:
            t = t * jnp.dot(sg, e_ref[:, HD * br:HD * (br + 1)], preferred_element_type=F32)
        mix = t if mix is None else mix + t
    upd = jnp.dot(mix.astype(BF16), w_ref[...], preferred_element_type=F32)
    xn = x_ref[0] + gate_ref[0] * upd
    out_ref[0] = xn
    if final:
        y = xn * lax.rsqrt(jnp.mean(xn * xn, axis=-1, keepdims=True) + RMS_EPS) * fg_ref[...]
        refs[pos + 1][0] = y


def _finish(x, gate, z, os_, w, gl=None, e=None, final_g=None, tm=512):
    nb, t, d = x.shape
    tm = min(tm, t)
    n_branch = len(os_)
    final = final_g is not None
    row = lambda c: pl.BlockSpec((1, tm, c), lambda n, i: (n, i, 0))
    if gate.shape[1] == 1:
        gate_spec = pl.BlockSpec((1, 1, d), lambda n, i: (n, 0, 0))
    else:
        gate_spec = row(d)
    in_specs = [row(d), gate_spec, row(z.shape[-1])]
    args = [x, gate, z]
    if n_branch > 1:
        in_specs += [row(gl.shape[-1]), pl.BlockSpec(e.shape, lambda n, i: (0, 0))]
        args += [gl, e]
    in_specs += [row(HD)] * n_branch + [pl.BlockSpec(w.shape, lambda n, i: (0, 0))]
    args += list(os_) + [w]
    out_specs, out_shape = [row(d)], [jax.ShapeDtypeStruct((nb, t, d), F32)]
    if final:
        in_specs.append(pl.BlockSpec((1, d), lambda n, i: (0, 0)))
        args.append(final_g.reshape(1, d))
        out_specs.append(row(d))
        out_shape.append(jax.ShapeDtypeStruct((nb, t, d), F32))
    res = pl.pallas_call(
        functools.partial(_finish_kernel, n_branch=n_branch, final=final),
        grid=(nb, t // tm),
        in_specs=in_specs, out_specs=out_specs, out_shape=out_shape,
        compiler_params=_params(("parallel", "parallel")),
        name="finish",
    )(*args)
    return res if final else res[0]


HALF_ROWS = 2 * CMP_BLOCK


def _compress_rows(xb, pos_ref, w1_ref, w2_ref):
    outs = []
    for k in range(2):
        hid = None
        for b in range(CMP_BLOCK):
            r = 2 * b + k
            xv = (xb(r) + pos_ref[r:r + 1, :]).astype(BF16)
            part = jnp.dot(xv, w1_ref[r], preferred_element_type=F32)
            hid = part if hid is None else hid + part
        outs.append(jnp.dot(_silu(hid).astype(BF16), w2_ref[k], preferred_element_type=F32))
    return jnp.concatenate(outs, axis=1)


def _compress_kernel(x_ref, pos_ref, w1_ref, w2_ref, o_ref, *, nblk):
    o_ref[0] = _compress_rows(lambda r: x_ref[0, pl.ds(r, nblk, stride=HALF_ROWS), :], pos_ref, w1_ref, w2_ref)


def _compress(rows, pos_e, w1e, w2e):
    nb, t, _ = rows.shape
    nblk = t // CMP_BLOCK
    return pl.pallas_call(
        functools.partial(_compress_kernel, nblk=nblk),
        grid=(nb,),
        in_specs=[pl.BlockSpec((1, 2 * t, KVD), lambda n: (n, 0, 0)),
                  pl.BlockSpec(pos_e.shape, lambda n: (0, 0)),
                  pl.BlockSpec(w1e.shape, lambda n: (0, 0, 0)),
                  pl.BlockSpec(w2e.shape, lambda n: (0, 0, 0))],
        out_specs=pl.BlockSpec((1, nblk, KV_ROW), lambda n: (n, 0, 0)),
        out_shape=jax.ShapeDtypeStruct((nb, nblk, KV_ROW), F32),
        compiler_params=_params(("parallel",)),
        name="compress",
    )(rows.reshape(nb, 2 * t, KVD), pos_e, w1e, w2e)


def _page_copies(pt_ref, pool_ref, raw_ref, sem_ref, n, slot, n_pages):
    return [pltpu.make_async_copy(pool_ref.at[pt_ref[n * n_pages + p]], raw_ref.at[slot, p], sem_ref.at[slot, p])
            for p in range(n_pages)]


def _compress_paged_kernel(pt_ref, pool_ref, pos_ref, w1_ref, w2_ref, o_ref, raw_ref, x_ref, sem_ref, *, n_pages):
    n = pl.program_id(0)
    slot = n % 2

    @pl.when(n == 0)
    def _():
        for cp in _page_copies(pt_ref, pool_ref, raw_ref, sem_ref, 0, 0, n_pages):
            cp.start()

    @pl.when(n + 1 < pl.num_programs(0))
    def _():
        for cp in _page_copies(pt_ref, pool_ref, raw_ref, sem_ref, n + 1, 1 - slot, n_pages):
            cp.start()

    for cp in _page_copies(pt_ref, pool_ref, raw_ref, sem_ref, n, slot, n_pages):
        cp.wait()

    def to_token_rows(p, carry):
        r0 = pl.multiple_of(p * PAGE_SIZE, PAGE_SIZE)
        for k in range(2):
            x_ref[k, pl.ds(r0, PAGE_SIZE), :] = raw_ref[slot, p, KVD * k:KVD * (k + 1), :].T
        return carry

    lax.fori_loop(0, n_pages, to_token_rows, 0)
    nblk = n_pages * PAGE_SIZE // CMP_BLOCK
    o_ref[0] = _compress_rows(lambda r: x_ref[r % 2, pl.ds(r // 2, nblk, stride=CMP_BLOCK), :],
                              pos_ref, w1_ref, w2_ref)


def _compress_paged(pool_t, page_table, pos_e, w1e, w2e):
    ns, n_pages = page_table.shape
    past = n_pages * PAGE_SIZE
    nblk = past // CMP_BLOCK
    grid_spec = pltpu.PrefetchScalarGridSpec(
        num_scalar_prefetch=1,
        grid=(ns,),
        in_specs=[pl.BlockSpec(memory_space=pl.ANY),
                  pl.BlockSpec(pos_e.shape, lambda n, pt: (0, 0)),
                  pl.BlockSpec(w1e.shape, lambda n, pt: (0, 0, 0)),
                  pl.BlockSpec(w2e.shape, lambda n, pt: (0, 0, 0))],
        out_specs=pl.BlockSpec((1, nblk, KV_ROW), lambda n, pt: (n, 0, 0)),
        scratch_shapes=[pltpu.VMEM((2, n_pages, KV_ROW, PAGE_SIZE), F32), pltpu.VMEM((2, past, KVD), F32),
                        pltpu.SemaphoreType.DMA((2, n_pages))])
    return pl.pallas_call(
        functools.partial(_compress_paged_kernel, n_pages=n_pages),
        grid_spec=grid_spec,
        out_shape=jax.ShapeDtypeStruct((ns, nblk, KV_ROW), F32),
        compiler_params=_params(("arbitrary",)),
        name="compress_paged",
    )(page_table.reshape(-1), pool_t, pos_e, w1e, w2e)


def _decode_queries(q):
    lane = lax.broadcasted_iota(jnp.int32, q.shape, 1)
    qs = q * ATTN_SCALE
    return [jnp.where(lane < HEAD_DIM, qs, 0.0), jnp.where(lane >= HEAD_DIM, qs, 0.0)]


def _decode_group(qg, kts, vts, biases, new_row, bias_new, new_on, sink):
    qb = qg.astype(BF16)
    ss = [jnp.dot(qb, kt, preferred_element_type=F32) + b for kt, b in zip(kts, biases)]
    s_new = jnp.sum(qg * new_row[:, :KVD], axis=-1, keepdims=True) + bias_new
    if new_on is not None:
        s_new = jnp.where(new_on, s_new, NEG)
    m = s_new
    for s in ss:
        m = jnp.maximum(m, jnp.max(s, axis=-1, keepdims=True))
    if sink is not None:
        m = jnp.maximum(m, sink)
    e_new = jnp.exp(s_new - m)
    den = e_new
    if sink is not None:
        den = den + jnp.exp(sink - m)
    o = e_new * new_row[:, KVD:]
    for s, vt in zip(ss, vts):
        e = jnp.exp(s - m)
        den = den + jnp.sum(e, axis=-1, keepdims=True)
        o = o + _nt_dot(e.astype(BF16), vt)
    return o / den


def _pair_values(o0, o1):
    lane = lax.broadcasted_iota(jnp.int32, (GROUP, LANES), 1)
    return jnp.where(lane < HEAD_DIM, o0, o1)


def _dec_attn_kernel(*refs, nb, use_sink):
    if use_sink:
        q_ref, kv_ref, new_ref, bias_ref, bnew_ref, sink_ref, o_ref = refs
    else:
        q_ref, kv_ref, new_ref, bias_ref, bnew_ref, o_ref = refs
    for b in range(nb):
        qgs = _decode_queries(q_ref[b].astype(F32))
        kt = kv_ref[b, :KVD, :].astype(BF16)
        vt = kv_ref[b, KVD:, :].astype(BF16)
        outs = [_decode_group(qgs[g], [kt], [vt], [bias_ref[g]], new_ref[b], bnew_ref[g], None,
                              sink_ref[g] if use_sink else None) for g in range(N_KV_HEADS)]
        o_ref[b] = _pair_values(outs[0], outs[1]).astype(o_ref.dtype)


def _dec_attn(q, buf_t, new, bias, bias_new, sink):
    ns, _, l = buf_t.shape
    nb = 8
    use_sink = sink is not None
    const = lambda a: pl.BlockSpec(a.shape, lambda i: (0,) * a.ndim)
    in_specs = [pl.BlockSpec((nb, GROUP, LANES), lambda i: (i, 0, 0)),
                pl.BlockSpec((nb, KV_ROW, l), lambda i: (i, 0, 0)),
                pl.BlockSpec((nb, 1, KV_ROW), lambda i: (i, 0, 0)),
                const(bias), const(bias_new)]
    args = [q, buf_t, new, bias, bias_new]
    if use_sink:
        in_specs.append(const(sink))
        args.append(sink)
    return pl.pallas_call(
        functools.partial(_dec_attn_kernel, nb=nb, use_sink=use_sink),
        grid=(ns // nb,),
        in_specs=in_specs,
        out_specs=pl.BlockSpec((nb, GROUP, LANES), lambda i: (i, 0, 0)),
        out_shape=jax.ShapeDtypeStruct((ns, GROUP, LANES), BF16),
        compiler_params=_params(("parallel",)),
        name="dec_attn",
    )(*args)


def _dec_cmp_topk_kernel(q_ref, ck_ref, cv_ref, o_ref, idx_ref, imp_ref, *, nb, nsp, qpos):
    t1 = jnp.full((1, 1), qpos, jnp.int32)
    negm, has_valid = _cmp_mask(t1, nsp)
    lane = lax.broadcasted_iota(jnp.int32, (GROUP, LANES), 1)
    for b in range(nb):
        q = q_ref[b].astype(F32) * ATTN_SCALE
        outs = []
        for g in range(N_KV_HEADS):
            qg = jnp.where((lane < HEAD_DIM) == (g == 0), q, 0.0).astype(BF16)
            p = _cmp_softmax(_nt_dot(qg, ck_ref[b]), negm, has_valid)
            outs.append(jnp.dot(p.astype(BF16), cv_ref[b], preferred_element_type=F32))
            ph = jnp.sum(p, axis=0, keepdims=True)
            imp_ref[N_KV_HEADS * b + g:N_KV_HEADS * b + g + 1, :] = ph[:, :nsp] + ph[:, nsp:]
        o_ref[b] = jnp.where(lane < HEAD_DIM, outs[0], outs[1]).astype(o_ref.dtype)
    rows = N_KV_HEADS * nb
    _, idx = _select_blocks(imp_ref[...], jnp.full((rows, 1), qpos, jnp.int32))
    idx_ref[0] = idx.astype(jnp.int32)


def _dec_cmp_topk(q, ck, cv, qpos):
    ns = q.shape[0]
    nb = 8
    nsp = ck.shape[1] // 2
    return pl.pallas_call(
        functools.partial(_dec_cmp_topk_kernel, nb=nb, nsp=nsp, qpos=qpos),
        grid=(ns // nb,),
        in_specs=[pl.BlockSpec((nb, GROUP, LANES), lambda i: (i, 0, 0)),
                  pl.BlockSpec((nb, 2 * nsp, KVD), lambda i: (i, 0, 0)),
                  pl.BlockSpec((nb, 2 * nsp, KVD), lambda i: (i, 0, 0))],
        out_specs=[pl.BlockSpec((nb, GROUP, LANES), lambda i: (i, 0, 0)),
                   pl.BlockSpec((1, N_KV_HEADS * nb, LANES), lambda i: (i, 0, 0))],
        out_shape=[jax.ShapeDtypeStruct((ns, GROUP, LANES), BF16),
                   jax.ShapeDtypeStruct((ns // nb, N_KV_HEADS * nb, LANES), jnp.int32)],
        scratch_shapes=[pltpu.VMEM((N_KV_HEADS * nb, nsp), F32)],
        compiler_params=_params(("parallel",)),
        name="dec_cmp_topk",
    )(q, ck, cv)


BLOCKS_PER_PAGE = PAGE_SIZE // SEL_BLOCK


def _sel_pick(idx_ref, n, g, k, n_past):
    blk = idx_ref[(n * N_KV_HEADS + g) * SEL_TOPK + k]
    return jnp.clip(blk, 0, n_past - 1), (blk >= 0) & (blk < n_past), blk == n_past


def _sel_copies(idx_ref, pt_ref, pool_ref, buf_ref, sem_ref, n, n_pages):
    copies = []
    for g in range(N_KV_HEADS):
        for k in range(SEL_TOPK):
            blk, _, _ = _sel_pick(idx_ref, n, g, k, n_pages * BLOCKS_PER_PAGE)
            phys = pt_ref[n * n_pages + blk // BLOCKS_PER_PAGE]
            copies.append(pltpu.make_async_copy(pool_ref.at[phys], buf_ref.at[g, k], sem_ref.at[g, k]))
    return copies


def _dec_sel_kernel(idx_ref, pt_ref, q_ref, new_ref, bt_ref, bnew_ref, pool_ref, o_ref, buf_ref, sem_ref, *,
                    n_pages):
    n = pl.program_id(0)
    n_past = n_pages * BLOCKS_PER_PAGE
    copies = _sel_copies(idx_ref, pt_ref, pool_ref, buf_ref, sem_ref, n, n_pages)
    for cp in copies:
        cp.start()
    lane_block = lax.broadcasted_iota(jnp.int32, (GROUP, LANES), 1) // SEL_BLOCK
    biases, new_on = [], []
    for g in range(N_KV_HEADS):
        has_new = False
        rows = []
        for k in range(SEL_TOPK):
            blk, cached, is_new = _sel_pick(idx_ref, n, g, k, n_past)
            want = jnp.where(cached, blk % BLOCKS_PER_PAGE, BLOCKS_PER_PAGE)
            row = bt_ref[blk // BLOCKS_PER_PAGE, GROUP * g:GROUP * (g + 1), :]
            rows.append(jnp.where(lane_block == want, row, NEG))
            has_new = has_new | is_new
        biases.append(rows)
        new_on.append(has_new)
    for cp in copies:
        cp.wait()
    qgs = _decode_queries(q_ref[0].astype(F32))
    outs = []
    for g in range(N_KV_HEADS):
        kts = [buf_ref[g, k, :KVD, :].astype(BF16) for k in range(SEL_TOPK)]
        vts = [buf_ref[g, k, KVD:, :].astype(BF16) for k in range(SEL_TOPK)]
        outs.append(_decode_group(qgs[g], kts, vts, biases[g], new_ref[0], bnew_ref[g], new_on[g], None))
    o_ref[0] = _pair_values(outs[0], outs[1]).astype(o_ref.dtype)


def _dec_sel(idx, page_table, q, new, bt, bias_new, pool_t):
    ns, n_pages = page_table.shape
    grid_spec = pltpu.PrefetchScalarGridSpec(
        num_scalar_prefetch=2,
        grid=(ns,),
        in_specs=[pl.BlockSpec((1, GROUP, LANES), lambda n, ix, pt: (n, 0, 0)),
                  pl.BlockSpec((1, 1, KV_ROW), lambda n, ix, pt: (n, 0, 0)),
                  pl.BlockSpec(bt.shape, lambda n, ix, pt: (0, 0, 0)),
                  pl.BlockSpec(bias_new.shape, lambda n, ix, pt: (0, 0, 0)),
                  pl.BlockSpec(memory_space=pl.ANY)],
        out_specs=pl.BlockSpec((1, GROUP, LANES), lambda n, ix, pt: (n, 0, 0)),
        scratch_shapes=[pltpu.VMEM((N_KV_HEADS, SEL_TOPK, KV_ROW, PAGE_SIZE), F32),
                        pltpu.SemaphoreType.DMA((N_KV_HEADS, SEL_TOPK))])
    return pl.pallas_call(
        functools.partial(_dec_sel_kernel, n_pages=n_pages),
        grid_spec=grid_spec,
        out_shape=jax.ShapeDtypeStruct((ns, GROUP, LANES), BF16),
        compiler_params=_params(("arbitrary",)),
        name="dec_sel",
    )(idx, page_table.reshape(-1), q, new, bt, bias_new, pool_t)


def _t5_bucket(dist):
    n = jnp.maximum(dist, 0)
    exact = N_BUCKETS // 2
    nf = jnp.maximum(n, exact).astype(F32)
    large = exact + (jnp.log(nf / exact) / math.log(T5_MAX_DISTANCE / exact) * (N_BUCKETS - exact)).astype(jnp.int32)
    return jnp.where(n < exact, n, jnp.minimum(large, N_BUCKETS - 1))


def _stack_heads(b):
    return b.reshape(N_KV_HEADS, GROUP * b.shape[1], b.shape[2])


def _band_bias(table, tq, window):
    dist = jnp.arange(tq)[:, None] - jnp.arange(window + tq)[None, :] + window
    b = jnp.transpose(table.astype(F32)[_t5_bucket(dist)], (2, 0, 1))
    return _stack_heads(jnp.where(((dist >= 0) & (dist <= window))[None], b, NEG))


def _near_bias(table, tq):
    assert tq + 1 >= T5_MAX_DISTANCE
    dist = jnp.arange(tq)[:, None] - jnp.arange(2 * tq)[None, :] + tq
    tab = table.astype(F32)
    b = jnp.transpose(tab[_t5_bucket(dist)] - tab[N_BUCKETS - 1], (2, 0, 1))
    return _stack_heads(jnp.where((dist >= 0)[None], b, NEG))


def _decode_bias(table, length, window):
    dist = length - jnp.arange(length)
    tab = table.astype(F32)
    b = jnp.where((dist <= window)[None], tab[_t5_bucket(dist)].T, NEG)
    return b.reshape(N_KV_HEADS, GROUP, length), tab[0].reshape(N_KV_HEADS, GROUP, 1)


def _stack_rows(v, tq):
    return jnp.broadcast_to(v.astype(F32).reshape(N_KV_HEADS, GROUP, 1, 1),
                            (N_KV_HEADS, GROUP, tq, LANES)).reshape(N_KV_HEADS, GROUP * tq, LANES)


def _pair_cols(w):
    lead = w.shape[:-1]
    w = w.reshape(lead + (N_KV_HEADS, GROUP, HEAD_DIM))
    return jnp.swapaxes(w, -3, -2).reshape(lead + (HD,))


def _pair_rows(w):
    w = w.reshape(N_KV_HEADS, GROUP, HEAD_DIM, w.shape[-1])
    return jnp.swapaxes(w, 0, 1).reshape(HD, w.shape[-1])


def _dim_token(x):
    nd = x.ndim
    x = jnp.transpose(x, tuple(range(nd - 4)) + (nd - 3, nd - 2, nd - 1, nd - 4))
    return x.reshape(x.shape[:nd - 4] + (KV_ROW, x.shape[-1]))


def _even_odd(comp, nsp):
    def half(x):
        return jnp.pad(x, ((0, 0), (0, nsp - x.shape[1]), (0, 0)))
    eo = jnp.concatenate([half(comp[:, 0::2]), half(comp[:, 1::2])], axis=1).astype(BF16)
    return eo[:, :, :KVD], eo[:, :, KVD:]


def _front_pad(x, rows):
    return jnp.pad(x, ((0, 0), (rows, 0), (0, 0)))


def kernel(x_prompt, x_sample, state_a_kv, cache_cmp_kv, cache_sel_kv, state_win_kv, page_table, c_prompt, c_sample, rel_bias_table, a_norm_g, a_w_ada, a_b_ada, a_w_in, a_sinks, a_w_out, kv_norm_g, kv_w_ada, kv_b_ada, kv_w_in, cmp_pos, cmp_w1, cmp_w2, b_norm_g, b_w_ada, b_b_ada, b_w_in, b_w_out, final_norm_g):
    n_p, seq, d = x_prompt.shape
    n_s = x_sample.shape[0]
    n_pages = page_table.shape[1]
    past = n_pages * PAGE_SIZE
    n_a, n_b = a_w_in.shape[0], b_w_in.shape[0]
    tq = Q_BLOCK
    assert x_sample.shape[1] == 1 and seq % 512 == 0 and n_s % 8 == 0
    assert seq // SEL_BLOCK >= SEL_TOPK and past // SEL_BLOCK >= SEL_TOPK
    assert state_a_kv.shape[2] == WINDOW_A and state_win_kv.shape[1] == WINDOW_B

    c_all = jnp.concatenate([c_prompt, jnp.zeros((8 - n_p, d), F32), c_sample], axis=0)
    ada_a = _ada(c_all, a_w_ada, a_b_ada)
    ada_b = _ada(c_all, b_w_ada, b_b_ada)
    ada_kv = _ada(c_all, kv_w_ada[None], kv_b_ada[None])

    def mods(ada, parts):
        pr = [ada[:n_p, None, i * d:(i + 1) * d] for i in range(parts)]
        sm = [ada[None, 8:8 + n_s, i * d:(i + 1) * d] for i in range(parts)]
        return pr, sm

    table = rel_bias_table
    xs = x_sample.reshape(1, n_s, d)
    xp = x_prompt

    bias_a = _band_bias(table, tq, WINDOW_A)
    dbias_a, dnew = _decode_bias(table, WINDOW_A, WINDOW_A)
    a_new_p, a_new_s = [], []
    for i in range(n_a):
        w_in = a_w_in[i]
        ws = [_pair_cols(w_in[:, :HD]).astype(BF16), w_in[:, HD:HD + KV_ROW].astype(BF16),
              _pair_cols(w_in[:, HD + KV_ROW:]).astype(BF16)]
        outs = [(0, 0, HD, BF16), (1, 0, KV_ROW, F32), (1, 0, KVD, BF16), (1, KVD, KV_ROW, BF16), (2, 0, HD, BF16)]
        w_out = _pair_rows(a_w_out[i]).astype(BF16)
        (sh_p, sc_p, gt_p), (sh_s, sc_s, gt_s) = mods(ada_a[i], 3)
        q, kv, kb, vb, z = _mod_proj(xp, a_norm_g[i], sh_p, sc_p, ws, outs, 512)
        vb0, vb1 = _with_ones(vb)
        o = _band_attn(q, _front_pad(kb, WINDOW_A), _front_pad(vb0, WINDOW_A), _front_pad(vb1, WINDOW_A), bias_a,
                       _stack_rows(a_sinks[i], tq), WINDOW_A)
        xp = _finish(xp, gt_p, z, [o], w_out)
        a_new_p.append(kv[:, seq - WINDOW_A:].reshape(n_p, WINDOW_A, 2, N_KV_HEADS, HEAD_DIM))
        q, kv, _, _, z = _mod_proj(xs, a_norm_g[i], sh_s, sc_s, ws, outs, 128)
        new = kv.reshape(n_s, 1, KV_ROW)
        o = _dec_attn(q.reshape(n_s, GROUP, LANES), _dim_token(state_a_kv[i]), new, dbias_a, dnew,
                      a_sinks[i].astype(F32).reshape(N_KV_HEADS, GROUP, 1))
        xs = _finish(xs, gt_s, z, [o.reshape(1, n_s, HD)], w_out)
        a_new_s.append(jnp.concatenate([state_a_kv[i][:, 1:], kv.reshape(n_s, 1, 2, N_KV_HEADS, HEAD_DIM)], axis=1))

    (sh_p, sc_p), (sh_s, sc_s) = mods(ada_kv[0], 2)
    w_kv = [kv_w_in.astype(BF16)]
    outs = [(0, KV_ROW * b, KV_ROW * (b + 1), F32) for b in range(N_BRANCH)]
    outs += [(0, KV_ROW + KVD * h, KV_ROW + KVD * (h + 1), BF16) for h in range(4)]
    cmp_p, sel_p, win_p, sel_kp, sel_vp, win_kp, win_vp = _mod_proj(xp, kv_norm_g, sh_p, sc_p, w_kv, outs, 512)
    cmp_s, sel_s, win_s = _mod_proj(xs, kv_norm_g, sh_s, sc_s, w_kv, outs[:N_BRANCH], 128)
    cmp_s, sel_s, win_s = (r.reshape(n_s, 1, KV_ROW) for r in (cmp_s, sel_s, win_s))

    eye_g = jnp.eye(N_KV_HEADS, dtype=F32)
    w1e = jnp.einsum('kbdh,gG->bkgdGh', cmp_w1, eye_g).reshape(
        HALF_ROWS, KVD, N_KV_HEADS * CMP_HIDDEN).astype(BF16)
    w2e = jnp.einsum('khd,gG->kghGd', cmp_w2, eye_g).reshape(2, N_KV_HEADS * CMP_HIDDEN, KVD).astype(BF16)
    pos_e = jnp.broadcast_to(cmp_pos[:, :, None, :], (CMP_BLOCK, 2, N_KV_HEADS, HEAD_DIM)).reshape(HALF_ROWS, KVD)
    comp_p = _compress(cmp_p, pos_e, w1e, w2e)
    comp_past = _compress_paged(_dim_token(cache_cmp_kv), page_table, pos_e, w1e, w2e)
    tail = jnp.pad(cmp_s, ((0, 0), (0, SEL_BLOCK - 1), (0, 0))).reshape(1, n_s * SEL_BLOCK, KV_ROW)
    comp_tail = _compress(tail, pos_e, w1e, w2e).reshape(n_s, SEL_BLOCK // CMP_BLOCK, KV_ROW)
    comp_s = jnp.concatenate([comp_past, comp_tail], axis=1)

    nsp_p = -(-(seq // SEL_BLOCK) // LANES) * LANES
    nsp_s = -(-(past // SEL_BLOCK + 1) // LANES) * LANES
    ck_p, cv_p = _even_odd(comp_p, nsp_p)
    ck_s, cv_s = _even_odd(comp_s, nsp_s)

    sel_vp0, sel_vp1 = (_front_pad(v, tq) for v in _with_ones(sel_vp))
    win_vp0, win_vp1 = (_front_pad(v, WINDOW_B) for v in _with_ones(win_vp))
    sel_kp, win_kp = _front_pad(sel_kp, tq), _front_pad(win_kp, WINDOW_B)
    key_block = (jnp.arange(tq + seq) - tq) // SEL_BLOCK
    et = ((key_block[:, None] == jnp.arange(nsp_p)[None, :]) & (jnp.arange(tq + seq) >= tq)[:, None]).astype(BF16)
    nb_sel = _near_bias(table, tq)
    bias_b = _band_bias(table, tq, WINDOW_B)
    dbias_b, _ = _decode_bias(table, WINDOW_B, WINDOW_B)
    win_buf_t = _dim_token(state_win_kv)
    kpos = jnp.arange(past).reshape(n_pages, PAGE_SIZE)
    bt = jnp.transpose(table.astype(F32)[_t5_bucket(past - kpos)], (0, 2, 1))
    sel_pool_t = _dim_token(cache_sel_kv)

    head_of = _PERM // HEAD_DIM
    e_np = np.zeros((LANES, N_BRANCH * HD), np.float32)
    for br in range(N_BRANCH):
        e_np[br * N_HEADS + head_of, br * HD + np.arange(HD)] = 1.0
    e_gl = jnp.asarray(e_np, dtype=BF16)

    y_p = y_s = None
    for j in range(n_b):
        w_in = b_w_in[j]
        w_z = _pair_cols(w_in[:, HD:HD + N_BRANCH * HD].reshape(d, N_BRANCH, HD)).reshape(d, N_BRANCH * HD)
        ws = [_pair_cols(w_in[:, :HD]).astype(BF16), w_z.astype(BF16),
              jnp.pad(w_in[:, HD + N_BRANCH * HD:], ((0, 0), (0, LANES - N_BRANCH * N_HEADS))).astype(BF16)]
        outs = [(0, 0, HD, BF16), (1, 0, N_BRANCH * HD, BF16), (2, 0, LANES, F32)]
        w_out = _pair_rows(b_w_out[j]).astype(BF16)
        fg = final_norm_g if j == n_b - 1 else None
        (sh_p, sc_p, gt_p), (sh_s, sc_s, gt_s) = mods(ada_b[j], 3)
        q, z, gl = _mod_proj(xp, b_norm_g[j], sh_p, sc_p, ws, outs, 256)
        o_cmp, sel = _cmp_topk(q, ck_p, cv_p)
        o_sel = _sel_attn(q, sel, sel_kp, sel_vp0, sel_vp1, et, nb_sel)
        o_win = _band_attn(q, win_kp, win_vp0, win_vp1, bias_b, None, WINDOW_B)
        r = _finish(xp, gt_p, z, [o_cmp, o_sel, o_win], w_out, gl, e_gl, fg, 256)
        xp, y_p = r if fg is not None else (r, None)
        q, z, gl = _mod_proj(xs, b_norm_g[j], sh_s, sc_s, ws, outs, 128)
        q = q.reshape(n_s, GROUP, LANES)
        o_cmp, idx = _dec_cmp_topk(q, ck_s, cv_s, past)
        idx = idx.reshape(n_s, N_KV_HEADS, LANES)[:, :, :SEL_TOPK].reshape(-1)
        o_sel = _dec_sel(idx, page_table, q, sel_s, bt, dnew, sel_pool_t)
        o_win = _dec_attn(q, win_buf_t, win_s, dbias_b, dnew, None)
        r = _finish(xs, gt_s, z, [o.reshape(1, n_s, HD) for o in (o_cmp, o_sel, o_win)], w_out, gl, e_gl, fg, 128)
        xs, y_s = r if fg is not None else (r, None)

    kv5 = lambda r: r.reshape(r.shape[0], r.shape[1], 2, N_KV_HEADS, HEAD_DIM)
    new_win_s = jnp.concatenate([state_win_kv[:, 1:], kv5(win_s)], axis=1)
    return (y_p, y_s.reshape(n_s, 1, d), jnp.stack(a_new_p), jnp.stack(a_new_s), kv5(cmp_p), kv5(cmp_s),
            kv5(sel_p), kv5(sel_s), kv5(win_p[:, seq - WINDOW_B:]), new_win_s)
```

```python
import functools
import math

import numpy as np
import jax
import jax.numpy as jnp
from jax import lax
from jax.experimental import pallas as pl
from jax.experimental.pallas import tpu as pltpu

F32, BF16 = jnp.float32, jnp.bfloat16

D_MODEL = 1024
N_HEADS = 16
HEAD_DIM = 64
N_KV_HEADS = 2
GROUP = N_HEADS // N_KV_HEADS
HD = N_HEADS * HEAD_DIM
KVD = N_KV_HEADS * HEAD_DIM
KV_ROW = 2 * KVD
WINDOW_A = 128
WINDOW_B = 512
Q_BLOCK = 128
CMP_BLOCK = 32
SEL_BLOCK = 64
SEL_TOPK = 16
CMP_HIDDEN = 2 * HEAD_DIM
N_BRANCH = 3
N_BUCKETS = 32
T5_MAX_DISTANCE = 128
RMS_EPS = 1e-6
ATTN_SCALE = HEAD_DIM ** -0.5
PAGE_SIZE = 128
LANES = 128
NEG = -1e30
VMEM_LIMIT = 56 * 1024 * 1024

_PERM = np.array([(GROUP * g + c) * HEAD_DIM + d for c in range(GROUP) for g in range(N_KV_HEADS)
                  for d in range(HEAD_DIM)], dtype=np.int32)


def _params(sem):
    return pltpu.CompilerParams(dimension_semantics=sem, vmem_limit_bytes=VMEM_LIMIT)


def _silu(x):
    return x * jax.nn.sigmoid(x)


def _ada_kernel(c_ref, w_ref, b_ref, o_ref):
    a = _silu(c_ref[...]).astype(BF16)
    o_ref[0] = jnp.dot(a, w_ref[0].astype(BF16), preferred_element_type=F32) + b_ref[0]


def _ada(c_all, w, b):
    nl, d, n = w.shape
    m = c_all.shape[0]
    tn = 512
    return pl.pallas_call(
        _ada_kernel,
        grid=(nl, n // tn),
        in_specs=[pl.BlockSpec((m, d), lambda l, j: (0, 0)),
                  pl.BlockSpec((1, d, tn), lambda l, j: (l, 0, j)),
                  pl.BlockSpec((1, 1, tn), lambda l, j: (l, 0, j))],
        out_specs=pl.BlockSpec((1, m, tn), lambda l, j: (l, 0, j)),
        out_shape=jax.ShapeDtypeStruct((nl, m, n), F32),
        compiler_params=_params(("parallel", "parallel")),
        name="ada_params",
    )(c_all, w, b.reshape(nl, 1, n))


def _mod_proj_kernel(x_ref, g_ref, sh_ref, sc_ref, *refs, nw, outs):
    w_refs, o_refs = refs[:nw], refs[nw:]
    x = x_ref[0]
    y = x * lax.rsqrt(jnp.mean(x * x, axis=-1, keepdims=True) + RMS_EPS) * g_ref[...]
    h = (y * (1.0 + sc_ref[0]) + sh_ref[0]).astype(BF16)
    res = {}
    for (wi, c0, c1, dt), o_ref in zip(outs, o_refs):
        if wi not in res:
            res[wi] = jnp.dot(h, w_refs[wi][...], preferred_element_type=F32)
        o_ref[0] = res[wi][:, c0:c1].astype(dt)


def _mod_proj(x, g, shift, scale, ws, outs, tm):
    nb, t, d = x.shape
    tm = min(tm, t)
    if shift.shape[1] == 1:
        mod_spec = pl.BlockSpec((1, 1, d), lambda n, i: (n, 0, 0))
    else:
        mod_spec = pl.BlockSpec((1, tm, d), lambda n, i: (n, i, 0))
    in_specs = [pl.BlockSpec((1, tm, d), lambda n, i: (n, i, 0)),
                pl.BlockSpec((1, d), lambda n, i: (0, 0)), mod_spec, mod_spec]
    in_specs += [pl.BlockSpec(w.shape, lambda n, i: (0, 0)) for w in ws]
    out_specs = [pl.BlockSpec((1, tm, c1 - c0), lambda n, i: (n, i, 0)) for (_, c0, c1, _) in outs]
    out_shape = [jax.ShapeDtypeStruct((nb, t, c1 - c0), dt) for (_, c0, c1, dt) in outs]
    return pl.pallas_call(
        functools.partial(_mod_proj_kernel, nw=len(ws), outs=tuple(outs)),
        grid=(nb, t // tm),
        in_specs=in_specs, out_specs=out_specs, out_shape=out_shape,
        compiler_params=_params(("parallel", "parallel")),
        name="mod_proj",
    )(x, g.reshape(1, d), shift, scale, *ws)


def _stack_queries(q_ref, qs_ref, tq):
    lane = lax.broadcasted_iota(jnp.int32, (tq, LANES), 1)
    for c in range(GROUP):
        qc = (q_ref[0, :, LANES * c:LANES * (c + 1)].astype(F32) * ATTN_SCALE)
        qs_ref[0, c * tq:(c + 1) * tq, :] = jnp.where(lane < HEAD_DIM, qc, 0.0).astype(BF16)
        qs_ref[1, c * tq:(c + 1) * tq, :] = jnp.where(lane >= HEAD_DIM, qc, 0.0).astype(BF16)


def _store_paired(o_ref, o0, o1, tq):
    lane = lax.broadcasted_iota(jnp.int32, o0.shape, 1)
    oc = jnp.where(lane < HEAD_DIM, o0, o1)
    for c in range(GROUP):
        o_ref[0, :, LANES * c:LANES * (c + 1)] = oc[c * tq:(c + 1) * tq].astype(o_ref.dtype)


def _nt_dot(a, b):
    return lax.dot_general(a, b, (((1,), (1,)), ((), ())), preferred_element_type=F32)


def _row_max(s):
    cm = s[:, :LANES]
    for c in range(1, s.shape[1] // LANES):
        cm = jnp.maximum(cm, s[:, LANES * c:LANES * (c + 1)])
    return jnp.max(cm, axis=-1, keepdims=True)


def _lanes(m, width):
    return m if width == LANES else jnp.concatenate([m] * (width // LANES), axis=1)


def _normalize(acc):
    return acc / pltpu.roll(acc, HEAD_DIM, axis=1)


def _with_ones(v):
    lane = lax.broadcasted_iota(jnp.int32, v.shape, v.ndim - 1)
    one = jnp.ones_like(v)
    return jnp.where(lane < HEAD_DIM, v, one), jnp.where(lane >= HEAD_DIM, v, one)


BAND_TQ = 256

def _band_attn_kernel(*refs, tq, window, use_sink):
    if use_sink:
        q_ref, k_ref, v0_ref, v1_ref, bias_ref, sink_ref, o_ref, qs_ref = refs
    else:
        q_ref, k_ref, v0_ref, v1_ref, bias_ref, o_ref, qs_ref = refs
    s0 = pl.multiple_of(pl.program_id(1) * tq, tq)
    band = window + tq
    rows = GROUP * tq
    _stack_queries(q_ref, qs_ref, tq)
    kb = k_ref[0, pl.ds(s0, band), :]
    kpos = s0 - window + lax.broadcasted_iota(jnp.int32, (1, band), 1)
    front = jnp.where(kpos >= 0, 0.0, NEG)
    lane = lax.broadcasted_iota(jnp.int32, (rows, LANES), 1)
    outs = []
    for g, v_ref in enumerate((v0_ref, v1_ref)):
        s = _nt_dot(qs_ref[g], kb) + bias_ref[g] + front
        m = jnp.broadcast_to(_row_max(s), (rows, LANES))
        if use_sink:
            m = jnp.maximum(m, sink_ref[g])
        e = jnp.exp(s - _lanes(m, band))
        acc = jnp.dot(e.astype(BF16), v_ref[0, pl.ds(s0, band), :], preferred_element_type=F32)
        if use_sink:
            acc = acc + jnp.where((lane < HEAD_DIM) == (g == 0), 0.0, jnp.exp(sink_ref[g] - m))
        outs.append(_normalize(acc))
    _store_paired(o_ref, outs[0], outs[1], tq)


def _band_attn(q, kp, vp0, vp1, bias, sink, window):
    n, s, _ = q.shape
    tq = bias.shape[1] // GROUP
    band = window + tq
    use_sink = sink is not None
    once = pl.Buffered(1)
    kv_spec = pl.BlockSpec((1, window + s, KVD), lambda b, i: (b, 0, 0))
    in_specs = [pl.BlockSpec((1, tq, HD), lambda b, i: (b, i, 0)), kv_spec, kv_spec, kv_spec,
                pl.BlockSpec((N_KV_HEADS, GROUP * tq, band), lambda b, i: (0, 0, 0), pipeline_mode=once)]
    args = [q, kp, vp0, vp1, bias]
    if use_sink:
        in_specs.append(pl.BlockSpec((N_KV_HEADS, GROUP * tq, LANES), lambda b, i: (0, 0, 0), pipeline_mode=once))
        args.append(sink)
    return pl.pallas_call(
        functools.partial(_band_attn_kernel, tq=tq, window=window, use_sink=use_sink),
        grid=(n, s // tq),
        in_specs=in_specs,
        out_specs=pl.BlockSpec((1, tq, HD), lambda b, i: (b, i, 0)),
        out_shape=jax.ShapeDtypeStruct((n, s, HD), BF16),
        scratch_shapes=[pltpu.VMEM((N_KV_HEADS, GROUP * tq, LANES), BF16)],
        compiler_params=_params(("parallel", "arbitrary")),
        name="band_attn",
    )(*args)


def _cmp_softmax(s3, negm, has_valid):
    def fold(x, op):
        acc = x[..., :LANES]
        for c in range(1, x.shape[-1] // LANES):
            acc = op(acc, x[..., LANES * c:LANES * (c + 1)])
        return acc

    s3 = s3 + negm
    m = jnp.max(fold(s3, jnp.maximum), axis=-1, keepdims=True)
    m = jnp.where(has_valid, m, 0.0)
    e = jnp.exp(s3 - m)
    den = jnp.sum(fold(e, jnp.add), axis=-1, keepdims=True)
    return e / jnp.where(den > 0, den, 1.0)


def _cmp_mask(t, nsp):
    u = lax.broadcasted_iota(jnp.int32, (t.shape[0], 2 * nsp), 1)
    cend = jnp.where(u < nsp, SEL_BLOCK * u + (CMP_BLOCK - 1), SEL_BLOCK * (u - nsp) + (SEL_BLOCK - 1))
    return jnp.where(cend <= t, 0.0, NEG), t >= CMP_BLOCK - 1


def _select_blocks(imp, t):
    rows, nsp = imp.shape
    j = lax.broadcasted_iota(jnp.int32, (rows, nsp), 1)
    jf = j.astype(F32)
    cur = t // SEL_BLOCK
    forced = (j == 0) | (j == cur) | (j == cur - 1)
    work = jnp.where(forced, jnp.inf, jnp.where(j <= cur, imp, -jnp.inf))
    sel = jnp.zeros((rows, nsp), F32)
    lane = lax.broadcasted_iota(jnp.int32, (rows, LANES), 1)
    idx = jnp.full((rows, LANES), -1.0, F32)
    for k in range(SEL_TOPK):
        m = jnp.max(work, axis=-1, keepdims=True)
        first = jnp.min(jnp.where(work == m, jf, float(nsp)), axis=-1, keepdims=True)
        hit = jf == first
        ok = m > -jnp.inf
        sel = jnp.where(hit & ok, 1.0, sel)
        idx = jnp.where(lane == k, jnp.where(ok, first, -1.0), idx)
        work = jnp.where(hit, -jnp.inf, work)
    return sel, idx


def _select_blocks_t(imp_t, t):
    nsp, cols = imp_t.shape
    j = lax.broadcasted_iota(jnp.int32, (nsp, cols), 0)
    jf = j.astype(F32)
    cur = t // SEL_BLOCK
    forced = (j == 0) | (j == cur) | (j == cur - 1)
    work = jnp.where(forced, jnp.inf, jnp.where(j <= cur, imp_t, -jnp.inf))
    sel = jnp.zeros((nsp, cols), F32)
    for _ in range(SEL_TOPK):
        m = jnp.max(work, axis=0, keepdims=True)
        first = jnp.min(jnp.where(work == m, jf, float(nsp)), axis=0, keepdims=True)
        hit = jf == first
        sel = jnp.where(hit & (m > -jnp.inf), 1.0, sel)
        work = jnp.where(hit, -jnp.inf, work)
    return sel


def _cmp_topk_kernel(q_ref, ck_ref, cv_ref, o_ref, sel_ref, qs_ref, *, tq, nsp):
    s0 = pl.program_id(1) * tq
    _stack_queries(q_ref, qs_ref, tq)
    t = s0 + lax.broadcasted_iota(jnp.int32, (tq, 1), 0)
    t_row = s0 + lax.broadcasted_iota(jnp.int32, (1, tq), 1)
    negm, has_valid = _cmp_mask(t, nsp)
    ck, cv = ck_ref[0], cv_ref[0]
    outs = []
    for g in range(N_KV_HEADS):
        s3 = _nt_dot(qs_ref[g], ck).reshape(GROUP, tq, 2 * nsp)
        p = _cmp_softmax(s3, negm[None], has_valid[None])
        outs.append(jnp.dot(p.reshape(GROUP * tq, 2 * nsp).astype(BF16), cv, preferred_element_type=F32))
        ph = jnp.sum(p, axis=0)
        sel_t = _select_blocks_t((ph[:, :nsp] + ph[:, nsp:]).T, t_row)
        sel_ref[0, g] = sel_t.T.astype(sel_ref.dtype)
    _store_paired(o_ref, outs[0], outs[1], tq)


def _cmp_topk(q, ck, cv):
    n, s, _ = q.shape
    tq = Q_BLOCK
    nsp = ck.shape[1] // 2
    return pl.pallas_call(
        functools.partial(_cmp_topk_kernel, tq=tq, nsp=nsp),
        grid=(n, s // tq),
        in_specs=[pl.BlockSpec((1, tq, HD), lambda b, i: (b, i, 0)),
                  pl.BlockSpec((1, 2 * nsp, KVD), lambda b, i: (b, 0, 0)),
                  pl.BlockSpec((1, 2 * nsp, KVD), lambda b, i: (b, 0, 0))],
        out_specs=[pl.BlockSpec((1, tq, HD), lambda b, i: (b, i, 0)),
                   pl.BlockSpec((1, N_KV_HEADS, tq, nsp), lambda b, i: (b, 0, i, 0))],
        out_shape=[jax.ShapeDtypeStruct((n, s, HD), BF16),
                   jax.ShapeDtypeStruct((n, N_KV_HEADS, s, nsp), BF16)],
        scratch_shapes=[pltpu.VMEM((N_KV_HEADS, GROUP * tq, LANES), BF16)],
        compiler_params=_params(("parallel", "arbitrary")),
        name="cmp_topk",
    )(q, ck, cv)


FAR_TILES = 4


def _sel_attn_kernel(q_ref, sel_ref, k_ref, v0_ref, v1_ref, et_ref, nb_ref, o_ref, qs_ref, m_ref, acc_ref, *, tq):
    qt = pl.program_id(1)
    s0 = pl.multiple_of(qt * tq, tq)
    rows = GROUP * tq
    v_refs = (v0_ref, v1_ref)
    _stack_queries(q_ref, qs_ref, tq)

    def key_tile(g, off, width, bias, first):
        kf = k_ref[0, pl.ds(off, width), :]
        vf = v_refs[g][0, pl.ds(off, width), :]
        allow = _nt_dot(sel_ref[0, g], et_ref[pl.ds(off, width), :])
        s3 = _nt_dot(qs_ref[g], kf).reshape(GROUP, tq, width) + ((allow - 1.0) * (-NEG))[None]
        if bias is not None:
            s3 = s3 + bias.reshape(GROUP, tq, width)
        s = s3.reshape(rows, width)
        rm = _row_max(s)
        if first:
            m_new = jnp.broadcast_to(rm, (rows, LANES))
        else:
            m_old = m_ref[g]
            m_new = jnp.maximum(m_old, rm)
        pv = jnp.dot(jnp.exp(s - _lanes(m_new, width)).astype(BF16), vf, preferred_element_type=F32)
        acc_ref[g] = pv if first else jnp.exp(m_old - m_new) * acc_ref[g] + pv
        m_ref[g] = m_new

    for g in range(N_KV_HEADS):
        key_tile(g, s0, 2 * tq, nb_ref[g], True)

    n_far = jnp.maximum(qt - 1, 0)
    n_wide = n_far // FAR_TILES

    def wide(j, carry):
        off = pl.multiple_of(tq + j * (FAR_TILES * tq), tq)
        for g in range(N_KV_HEADS):
            key_tile(g, off, FAR_TILES * tq, None, False)
        return carry

    def narrow(i, carry):
        off = pl.multiple_of(tq + (n_wide * FAR_TILES + i) * tq, tq)
        for g in range(N_KV_HEADS):
            key_tile(g, off, tq, None, False)
        return carry

    lax.fori_loop(0, n_wide, wide, 0)
    lax.fori_loop(0, n_far - n_wide * FAR_TILES, narrow, 0)
    _store_paired(o_ref, _normalize(acc_ref[0]), _normalize(acc_ref[1]), tq)


def _sel_attn(q, sel, kp, vp0, vp1, et, nb):
    n, s, _ = q.shape
    tq = Q_BLOCK
    nsp = sel.shape[-1]
    kv_spec = pl.BlockSpec((1, tq + s, KVD), lambda b, i: (b, 0, 0))
    return pl.pallas_call(
        functools.partial(_sel_attn_kernel, tq=tq),
        grid=(n, s // tq),
        in_specs=[pl.BlockSpec((1, tq, HD), lambda b, i: (b, i, 0)),
                  pl.BlockSpec((1, N_KV_HEADS, tq, nsp), lambda b, i: (b, 0, i, 0)),
                  kv_spec, kv_spec, kv_spec,
                  pl.BlockSpec((tq + s, nsp), lambda b, i: (0, 0)),
                  pl.BlockSpec((N_KV_HEADS, GROUP * tq, 2 * tq), lambda b, i: (0, 0, 0))],
        out_specs=pl.BlockSpec((1, tq, HD), lambda b, i: (b, i, 0)),
        out_shape=jax.ShapeDtypeStruct((n, s, HD), BF16),
        scratch_shapes=[pltpu.VMEM((N_KV_HEADS, GROUP * tq, LANES), BF16),
                        pltpu.VMEM((N_KV_HEADS, GROUP * tq, LANES), F32),
                        pltpu.VMEM((N_KV_HEADS, GROUP * tq, LANES), F32)],
        compiler_params=_params(("parallel", "arbitrary")),
        name="sel_attn",
    )(q, sel, kp, vp0, vp1, et, nb)


def _finish_kernel(*refs, n_branch, final):
    x_ref, gate_ref, z_ref = refs[:3]
    pos = 3
    if n_branch > 1:
        gl_ref, e_ref = refs[pos:pos + 2]
        pos += 2
    o_refs = refs[pos:pos + n_branch]
    pos += n_branch
    w_ref = refs[pos]
    pos += 1
    if final:
        fg_ref = refs[pos]
        pos += 1
    out_ref = refs[pos]
    mix = None
    if n_branch > 1:
        sg = jax.nn.sigmoid(gl_ref[0]).astype(BF16)
    for br in range(n_branch):
        t = o_refs[br][0].astype(F32) * _silu(z_ref[0, :, HD * br:HD * (br + 1)].astype(F32))
        if n_branch > 1:
            t = t * jnp.dot(sg, e_ref[:, HD * br:HD * (br + 1)], preferred_element_type=F32)
        mix = t if mix is None else mix + t
    upd = jnp.dot(mix.astype(BF16), w_ref[...], preferred_element_type=F32)
    xn = x_ref[0] + gate_ref[0] * upd
    out_ref[0] = xn
    if final:
        y = xn * lax.rsqrt(jnp.mean(xn * xn, axis=-1, keepdims=True) + RMS_EPS) * fg_ref[...]
        refs[pos + 1][0] = y


def _finish(x, gate, z, os_, w, gl=None, e=None, final_g=None, tm=512):
    nb, t, d = x.shape
    tm = min(tm, t)
    n_branch = len(os_)
    final = final_g is not None
    row = lambda c: pl.BlockSpec((1, tm, c), lambda n, i: (n, i, 0))
    if gate.shape[1] == 1:
        gate_spec = pl.BlockSpec((1, 1, d), lambda n, i: (n, 0, 0))
    else:
        gate_spec = row(d)
    in_specs = [row(d), gate_spec, row(z.shape[-1])]
    args = [x, gate, z]
    if n_branch > 1:
        in_specs += [row(gl.shape[-1]), pl.BlockSpec(e.shape, lambda n, i: (0, 0))]
        args += [gl, e]
    in_specs += [row(HD)] * n_branch + [pl.BlockSpec(w.shape, lambda n, i: (0, 0))]
    args += list(os_) + [w]
    out_specs, out_shape = [row(d)], [jax.ShapeDtypeStruct((nb, t, d), F32)]
    if final:
        in_specs.append(pl.BlockSpec((1, d), lambda n, i: (0, 0)))
        args.append(final_g.reshape(1, d))
        out_specs.append(row(d))
        out_shape.append(jax.ShapeDtypeStruct((nb, t, d), F32))
    res = pl.pallas_call(
        functools.partial(_finish_kernel, n_branch=n_branch, final=final),
        grid=(nb, t // tm),
        in_specs=in_specs, out_specs=out_specs, out_shape=out_shape,
        compiler_params=_params(("parallel", "parallel")),
        name="finish",
    )(*args)
    return res if final else res[0]


HALF_ROWS = 2 * CMP_BLOCK


def _compress_rows(xb, pos_ref, w1_ref, w2_ref, side=None):
    outs = []
    for k in range(2):
        hid = None
        for b in range(CMP_BLOCK):
            r = 2 * b + k
            xv = (xb(r) + pos_ref[r:r + 1, :]).astype(BF16)
            part = jnp.dot(xv, w1_ref[r], preferred_element_type=F32)
            hid = part if hid is None else hid + part
            if side is not None:
                side(k * CMP_BLOCK + b)
        outs.append(jnp.dot(_silu(hid).astype(BF16), w2_ref[k], preferred_element_type=F32))
    return jnp.concatenate(outs, axis=1)


def _compress_kernel(x_ref, pos_ref, w1_ref, w2_ref, o_ref, *, nblk):
    o_ref[0] = _compress_rows(lambda r: x_ref[0, pl.ds(r, nblk, stride=HALF_ROWS), :], pos_ref, w1_ref, w2_ref)


def _compress(rows, pos_e, w1e, w2e):
    nb, t, _ = rows.shape
    nblk = t // CMP_BLOCK
    return pl.pallas_call(
        functools.partial(_compress_kernel, nblk=nblk),
        grid=(nb,),
        in_specs=[pl.BlockSpec((1, 2 * t, KVD), lambda n: (n, 0, 0)),
                  pl.BlockSpec(pos_e.shape, lambda n: (0, 0)),
                  pl.BlockSpec(w1e.shape, lambda n: (0, 0, 0)),
                  pl.BlockSpec(w2e.shape, lambda n: (0, 0, 0))],
        out_specs=pl.BlockSpec((1, nblk, KV_ROW), lambda n: (n, 0, 0)),
        out_shape=jax.ShapeDtypeStruct((nb, nblk, KV_ROW), F32),
        compiler_params=_params(("parallel",)),
        name="compress",
    )(rows.reshape(nb, 2 * t, KVD), pos_e, w1e, w2e)


def _page_copies(pt_ref, pool_ref, raw_ref, sem_ref, n, slot, n_pages):
    return [pltpu.make_async_copy(pool_ref.at[pt_ref[n * n_pages + p]], raw_ref.at[slot, p], sem_ref.at[slot, p])
            for p in range(n_pages)]


CMP_PITCH = CMP_BLOCK + 4
BLOCKS_PER_CMP_PAGE = PAGE_SIZE // CMP_BLOCK


def _compress_paged_kernel(pt_ref, pool_ref, pos_ref, w1_ref, w2_ref, o_ref, raw_ref, x_ref, sem_ref, *,
                           n_pages, n_samples):
    n = pl.program_id(0)
    cur = n % 2
    nxt = 1 - cur
    copies = lambda sample, slot: _page_copies(pt_ref, pool_ref, raw_ref, sem_ref, sample, slot, n_pages)

    def to_token_rows(slot, p):
        for k in range(2):
            t = raw_ref[slot, p, KVD * k:KVD * (k + 1), :].T
            for c in range(BLOCKS_PER_CMP_PAGE):
                row0 = (p * BLOCKS_PER_CMP_PAGE + c) * CMP_PITCH
                x_ref[slot, k, pl.ds(row0, CMP_BLOCK), :] = t[c * CMP_BLOCK:(c + 1) * CMP_BLOCK]

    @pl.when(n == 0)
    def _():
        for cp in copies(0, 0):
            cp.start()
        if n_samples > 1:
            for cp in copies(1, 1):
                cp.start()
        for cp in copies(0, 0):
            cp.wait()

        def first(p, carry):
            to_token_rows(0, p)
            return carry

        lax.fori_loop(0, n_pages, first, 0)

    @pl.when(n + 1 < n_samples)
    def _():
        for cp in copies(n + 1, nxt):
            cp.wait()

    @pl.when(n + 2 < n_samples)
    def _():
        for cp in copies(n + 2, cur):
            cp.start()

    per_call = -(-n_pages // HALF_ROWS)

    def side(i):
        for p in range(i * per_call, min((i + 1) * per_call, n_pages)):
            to_token_rows(nxt, p)

    nblk = n_pages * BLOCKS_PER_CMP_PAGE
    o_ref[0] = _compress_rows(lambda r: x_ref[cur, r % 2, pl.ds(r // 2, nblk, stride=CMP_PITCH), :],
                              pos_ref, w1_ref, w2_ref, side if n_samples > 1 else None)


def _compress_paged(pool_t, page_table, pos_e, w1e, w2e):
    ns, n_pages = page_table.shape
    past = n_pages * PAGE_SIZE
    nblk = past // CMP_BLOCK
    grid_spec = pltpu.PrefetchScalarGridSpec(
        num_scalar_prefetch=1,
        grid=(ns,),
        in_specs=[pl.BlockSpec(memory_space=pl.ANY),
                  pl.BlockSpec(pos_e.shape, lambda n, pt: (0, 0)),
                  pl.BlockSpec(w1e.shape, lambda n, pt: (0, 0, 0)),
                  pl.BlockSpec(w2e.shape, lambda n, pt: (0, 0, 0))],
        out_specs=pl.BlockSpec((1, nblk, KV_ROW), lambda n, pt: (n, 0, 0)),
        scratch_shapes=[pltpu.VMEM((2, n_pages, KV_ROW, PAGE_SIZE), F32),
                        pltpu.VMEM((2, 2, nblk * CMP_PITCH, KVD), F32),
                        pltpu.SemaphoreType.DMA((2, n_pages))])
    return pl.pallas_call(
        functools.partial(_compress_paged_kernel, n_pages=n_pages, n_samples=ns),
        grid_spec=grid_spec,
        out_shape=jax.ShapeDtypeStruct((ns, nblk, KV_ROW), F32),
        compiler_params=_params(("arbitrary",)),
        name="compress_paged",
    )(page_table.reshape(-1), pool_t, pos_e, w1e, w2e)


def _decode_queries(q):
    lane = lax.broadcasted_iota(jnp.int32, q.shape, 1)
    qs = q * ATTN_SCALE
    return [jnp.where(lane < HEAD_DIM, qs, 0.0), jnp.where(lane >= HEAD_DIM, qs, 0.0)]


def _decode_group(qg, kts, vts, biases, new_row, bias_new, new_on, sink):
    qb = qg.astype(BF16)
    ss = [jnp.dot(qb, kt, preferred_element_type=F32) + b for kt, b in zip(kts, biases)]
    s_new = jnp.sum(qg * new_row[:, :KVD], axis=-1, keepdims=True) + bias_new
    if new_on is not None:
        s_new = jnp.where(new_on, s_new, NEG)
    m = s_new
    for s in ss:
        m = jnp.maximum(m, jnp.max(s, axis=-1, keepdims=True))
    if sink is not None:
        m = jnp.maximum(m, sink)
    e_new = jnp.exp(s_new - m)
    den = e_new
    if sink is not None:
        den = den + jnp.exp(sink - m)
    o = e_new * new_row[:, KVD:]
    for s, vt in zip(ss, vts):
        e = jnp.exp(s - m)
        den = den + jnp.sum(e, axis=-1, keepdims=True)
        o = o + _nt_dot(e.astype(BF16), vt)
    return o / den


def _pair_values(o0, o1):
    lane = lax.broadcasted_iota(jnp.int32, (GROUP, LANES), 1)
    return jnp.where(lane < HEAD_DIM, o0, o1)


def _dec_attn_kernel(*refs, nb, use_sink):
    if use_sink:
        q_ref, kv_ref, new_ref, bias_ref, bnew_ref, sink_ref, o_ref = refs
    else:
        q_ref, kv_ref, new_ref, bias_ref, bnew_ref, o_ref = refs
    for b in range(nb):
        qgs = _decode_queries(q_ref[b].astype(F32))
        kt = kv_ref[b, :KVD, :].astype(BF16)
        vt = kv_ref[b, KVD:, :].astype(BF16)
        outs = [_decode_group(qgs[g], [kt], [vt], [bias_ref[g]], new_ref[b], bnew_ref[g], None,
                              sink_ref[g] if use_sink else None) for g in range(N_KV_HEADS)]
        o_ref[b] = _pair_values(outs[0], outs[1]).astype(o_ref.dtype)


def _dec_attn(q, buf_t, new, bias, bias_new, sink):
    ns, _, l = buf_t.shape
    nb = 8
    use_sink = sink is not None
    const = lambda a: pl.BlockSpec(a.shape, lambda i: (0,) * a.ndim)
    in_specs = [pl.BlockSpec((nb, GROUP, LANES), lambda i: (i, 0, 0)),
                pl.BlockSpec((nb, KV_ROW, l), lambda i: (i, 0, 0)),
                pl.BlockSpec((nb, 1, KV_ROW), lambda i: (i, 0, 0)),
                const(bias), const(bias_new)]
    args = [q, buf_t, new, bias, bias_new]
    if use_sink:
        in_specs.append(const(sink))
        args.append(sink)
    return pl.pallas_call(
        functools.partial(_dec_attn_kernel, nb=nb, use_sink=use_sink),
        grid=(ns // nb,),
        in_specs=in_specs,
        out_specs=pl.BlockSpec((nb, GROUP, LANES), lambda i: (i, 0, 0)),
        out_shape=jax.ShapeDtypeStruct((ns, GROUP, LANES), BF16),
        compiler_params=_params(("parallel",)),
        name="dec_attn",
    )(*args)


def _dec_cmp_topk_kernel(q_ref, ck_ref, cv_ref, o_ref, idx_ref, imp_ref, *, nb, nsp, qpos):
    t1 = jnp.full((1, 1), qpos, jnp.int32)
    negm, has_valid = _cmp_mask(t1, nsp)
    lane = lax.broadcasted_iota(jnp.int32, (GROUP, LANES), 1)
    for b in range(nb):
        q = q_ref[b].astype(F32) * ATTN_SCALE
        outs = []
        for g in range(N_KV_HEADS):
            qg = jnp.where((lane < HEAD_DIM) == (g == 0), q, 0.0).astype(BF16)
            p = _cmp_softmax(_nt_dot(qg, ck_ref[b]), negm, has_valid)
            outs.append(jnp.dot(p.astype(BF16), cv_ref[b], preferred_element_type=F32))
            ph = jnp.sum(p, axis=0, keepdims=True)
            imp_ref[N_KV_HEADS * b + g:N_KV_HEADS * b + g + 1, :] = ph[:, :nsp] + ph[:, nsp:]
        o_ref[b] = jnp.where(lane < HEAD_DIM, outs[0], outs[1]).astype(o_ref.dtype)
    rows = N_KV_HEADS * nb
    _, idx = _select_blocks(imp_ref[...], jnp.full((rows, 1), qpos, jnp.int32))
    idx_ref[0] = idx.astype(jnp.int32)


def _dec_cmp_topk(q, ck, cv, qpos):
    ns = q.shape[0]
    nb = 8
    nsp = ck.shape[1] // 2
    return pl.pallas_call(
        functools.partial(_dec_cmp_topk_kernel, nb=nb, nsp=nsp, qpos=qpos),
        grid=(ns // nb,),
        in_specs=[pl.BlockSpec((nb, GROUP, LANES), lambda i: (i, 0, 0)),
                  pl.BlockSpec((nb, 2 * nsp, KVD), lambda i: (i, 0, 0)),
                  pl.BlockSpec((nb, 2 * nsp, KVD), lambda i: (i, 0, 0))],
        out_specs=[pl.BlockSpec((nb, GROUP, LANES), lambda i: (i, 0, 0)),
                   pl.BlockSpec((1, N_KV_HEADS * nb, LANES), lambda i: (i, 0, 0))],
        out_shape=[jax.ShapeDtypeStruct((ns, GROUP, LANES), BF16),
                   jax.ShapeDtypeStruct((ns // nb, N_KV_HEADS * nb, LANES), jnp.int32)],
        scratch_shapes=[pltpu.VMEM((N_KV_HEADS * nb, nsp), F32)],
        compiler_params=_params(("parallel",)),
        name="dec_cmp_topk",
    )(q, ck, cv)


BLOCKS_PER_PAGE = PAGE_SIZE // SEL_BLOCK


def _sel_pick(idx_ref, n, g, k, n_past):
    blk = idx_ref[(n * N_KV_HEADS + g) * SEL_TOPK + k]
    return jnp.clip(blk, 0, n_past - 1), (blk >= 0) & (blk < n_past), blk == n_past


def _sel_copies(idx_ref, pt_ref, pool_ref, buf_ref, sem_ref, n, slot, n_pages):
    copies = []
    for g in range(N_KV_HEADS):
        for k in range(SEL_TOPK):
            blk, _, _ = _sel_pick(idx_ref, n, g, k, n_pages * BLOCKS_PER_PAGE)
            phys = pt_ref[n * n_pages + blk // BLOCKS_PER_PAGE]
            copies.append(pltpu.make_async_copy(pool_ref.at[phys], buf_ref.at[slot, g, k], sem_ref.at[slot, g, k]))
    return copies


def _dec_sel_kernel(idx_ref, pt_ref, q_ref, new_ref, bt_ref, bnew_ref, pool_ref, o_ref, buf_ref, sem_ref, *,
                    n_pages):
    n = pl.program_id(0)
    slot = n % 2
    n_past = n_pages * BLOCKS_PER_PAGE

    @pl.when(n == 0)
    def _():
        for cp in _sel_copies(idx_ref, pt_ref, pool_ref, buf_ref, sem_ref, 0, 0, n_pages):
            cp.start()

    @pl.when(n + 1 < pl.num_programs(0))
    def _():
        for cp in _sel_copies(idx_ref, pt_ref, pool_ref, buf_ref, sem_ref, n + 1, 1 - slot, n_pages):
            cp.start()

    copies = _sel_copies(idx_ref, pt_ref, pool_ref, buf_ref, sem_ref, n, slot, n_pages)
    lane_block = lax.broadcasted_iota(jnp.int32, (GROUP, LANES), 1) // SEL_BLOCK
    biases, new_on = [], []
    for g in range(N_KV_HEADS):
        has_new = False
        rows = []
        for k in range(SEL_TOPK):
            blk, cached, is_new = _sel_pick(idx_ref, n, g, k, n_past)
            want = jnp.where(cached, blk % BLOCKS_PER_PAGE, BLOCKS_PER_PAGE)
            row = bt_ref[blk // BLOCKS_PER_PAGE, GROUP * g:GROUP * (g + 1), :]
            rows.append(jnp.where(lane_block == want, row, NEG))
            has_new = has_new | is_new
        biases.append(rows)
        new_on.append(has_new)
    for cp in copies:
        cp.wait()
    qgs = _decode_queries(q_ref[0].astype(F32))
    outs = []
    for g in range(N_KV_HEADS):
        kts = [buf_ref[slot, g, k, :KVD, :].astype(BF16) for k in range(SEL_TOPK)]
        vts = [buf_ref[slot, g, k, KVD:, :].astype(BF16) for k in range(SEL_TOPK)]
        outs.append(_decode_group(qgs[g], kts, vts, biases[g], new_ref[0], bnew_ref[g], new_on[g], None))
    o_ref[0] = _pair_values(outs[0], outs[1]).astype(o_ref.dtype)


def _dec_sel(idx, page_table, q, new, bt, bias_new, pool_t):
    ns, n_pages = page_table.shape
    grid_spec = pltpu.PrefetchScalarGridSpec(
        num_scalar_prefetch=2,
        grid=(ns,),
        in_specs=[pl.BlockSpec((1, GROUP, LANES), lambda n, ix, pt: (n, 0, 0)),
                  pl.BlockSpec((1, 1, KV_ROW), lambda n, ix, pt: (n, 0, 0)),
                  pl.BlockSpec(bt.shape, lambda n, ix, pt: (0, 0, 0)),
                  pl.BlockSpec(bias_new.shape, lambda n, ix, pt: (0, 0, 0)),
                  pl.BlockSpec(memory_space=pl.ANY)],
        out_specs=pl.BlockSpec((1, GROUP, LANES), lambda n, ix, pt: (n, 0, 0)),
        scratch_shapes=[pltpu.VMEM((2, N_KV_HEADS, SEL_TOPK, KV_ROW, PAGE_SIZE), F32),
                        pltpu.SemaphoreType.DMA((2, N_KV_HEADS, SEL_TOPK))])
    return pl.pallas_call(
        functools.partial(_dec_sel_kernel, n_pages=n_pages),
        grid_spec=grid_spec,
        out_shape=jax.ShapeDtypeStruct((ns, GROUP, LANES), BF16),
        compiler_params=_params(("arbitrary",)),
        name="dec_sel",
    )(idx, page_table.reshape(-1), q, new, bt, bias_new, pool_t)


def _t5_bucket(dist):
    n = jnp.maximum(dist, 0)
    exact = N_BUCKETS // 2
    nf = jnp.maximum(n, exact).astype(F32)
    large = exact + (jnp.log(nf / exact) / math.log(T5_MAX_DISTANCE / exact) * (N_BUCKETS - exact)).astype(jnp.int32)
    return jnp.where(n < exact, n, jnp.minimum(large, N_BUCKETS - 1))


def _bias_of(table, dist):
    onehot = (_t5_bucket(dist)[..., None] == jnp.arange(N_BUCKETS)).astype(F32)
    return jnp.einsum('...b,bh->...h', onehot, table.astype(F32), precision=lax.Precision.HIGHEST)


def _stack_heads(b):
    return b.reshape(N_KV_HEADS, GROUP * b.shape[1], b.shape[2])


def _band_bias(table, tq, window):
    dist = jnp.arange(tq)[:, None] - jnp.arange(window + tq)[None, :] + window
    b = jnp.transpose(_bias_of(table, dist), (2, 0, 1))
    return _stack_heads(jnp.where(((dist >= 0) & (dist <= window))[None], b, NEG))


def _near_bias(table, tq):
    assert tq + 1 >= T5_MAX_DISTANCE
    dist = jnp.arange(tq)[:, None] - jnp.arange(2 * tq)[None, :] + tq
    tab = table.astype(F32)
    b = jnp.transpose(_bias_of(table, dist) - tab[N_BUCKETS - 1], (2, 0, 1))
    return _stack_heads(jnp.where((dist >= 0)[None], b, NEG))


def _decode_bias(table, length, window):
    dist = length - jnp.arange(length)
    tab = table.astype(F32)
    b = jnp.where((dist <= window)[None], _bias_of(table, dist).T, NEG)
    return b.reshape(N_KV_HEADS, GROUP, length), tab[0].reshape(N_KV_HEADS, GROUP, 1)


def _stack_rows(v, tq):
    return jnp.broadcast_to(v.astype(F32).reshape(N_KV_HEADS, GROUP, 1, 1),
                            (N_KV_HEADS, GROUP, tq, LANES)).reshape(N_KV_HEADS, GROUP * tq, LANES)


def _pair_cols(w):
    lead = w.shape[:-1]
    w = w.reshape(lead + (N_KV_HEADS, GROUP, HEAD_DIM))
    return jnp.swapaxes(w, -3, -2).reshape(lead + (HD,))


def _pair_rows(w):
    w = w.reshape(N_KV_HEADS, GROUP, HEAD_DIM, w.shape[-1])
    return jnp.swapaxes(w, 0, 1).reshape(HD, w.shape[-1])


def _dim_token(x):
    nd = x.ndim
    x = jnp.transpose(x, tuple(range(nd - 4)) + (nd - 3, nd - 2, nd - 1, nd - 4))
    return x.reshape(x.shape[:nd - 4] + (KV_ROW, x.shape[-1]))


def _even_odd(comp, nsp):
    def half(x):
        return jnp.pad(x, ((0, 0), (0, nsp - x.shape[1]), (0, 0)))
    eo = jnp.concatenate([half(comp[:, 0::2]), half(comp[:, 1::2])], axis=1).astype(BF16)
    return eo[:, :, :KVD], eo[:, :, KVD:]


def _front_pad(x, rows):
    return jnp.pad(x, ((0, 0), (rows, 0), (0, 0)))


def kernel(x_prompt, x_sample, state_a_kv, cache_cmp_kv, cache_sel_kv, state_win_kv, page_table, c_prompt, c_sample, rel_bias_table, a_norm_g, a_w_ada, a_b_ada, a_w_in, a_sinks, a_w_out, kv_norm_g, kv_w_ada, kv_b_ada, kv_w_in, cmp_pos, cmp_w1, cmp_w2, b_norm_g, b_w_ada, b_b_ada, b_w_in, b_w_out, final_norm_g):
    n_p, seq, d = x_prompt.shape
    n_s = x_sample.shape[0]
    n_pages = page_table.shape[1]
    past = n_pages * PAGE_SIZE
    n_a, n_b = a_w_in.shape[0], b_w_in.shape[0]
    tq = Q_BLOCK
    assert x_sample.shape[1] == 1 and seq % 512 == 0 and n_s % 8 == 0
    assert seq // SEL_BLOCK >= SEL_TOPK and past // SEL_BLOCK >= SEL_TOPK
    assert state_a_kv.shape[2] == WINDOW_A and state_win_kv.shape[1] == WINDOW_B

    c_all = jnp.concatenate([c_prompt, jnp.zeros((8 - n_p, d), F32), c_sample], axis=0)
    ada_a = _ada(c_all, a_w_ada, a_b_ada)
    ada_b = _ada(c_all, b_w_ada, b_b_ada)
    ada_kv = _ada(c_all, kv_w_ada[None], kv_b_ada[None])

    def mods(ada, parts):
        pr = [ada[:n_p, None, i * d:(i + 1) * d] for i in range(parts)]
        sm = [ada[None, 8:8 + n_s, i * d:(i + 1) * d] for i in range(parts)]
        return pr, sm

    table = rel_bias_table
    xs = x_sample.reshape(1, n_s, d)
    xp = x_prompt

    bias_a = _band_bias(table, BAND_TQ, WINDOW_A)
    dbias_a, dnew = _decode_bias(table, WINDOW_A, WINDOW_A)
    a_new_p, a_new_s = [], []
    for i in range(n_a):
        w_in = a_w_in[i]
        ws = [_pair_cols(w_in[:, :HD]).astype(BF16), w_in[:, HD:HD + KV_ROW].astype(BF16),
              _pair_cols(w_in[:, HD + KV_ROW:]).astype(BF16)]
        outs = [(0, 0, HD, BF16), (1, 0, KV_ROW, F32), (1, 0, KVD, BF16), (1, KVD, KV_ROW, BF16), (2, 0, HD, BF16)]
        w_out = _pair_rows(a_w_out[i]).astype(BF16)
        (sh_p, sc_p, gt_p), (sh_s, sc_s, gt_s) = mods(ada_a[i], 3)
        q, kv, kb, vb, z = _mod_proj(xp, a_norm_g[i], sh_p, sc_p, ws, outs, 512)
        vb0, vb1 = _with_ones(vb)
        o = _band_attn(q, _front_pad(kb, WINDOW_A), _front_pad(vb0, WINDOW_A), _front_pad(vb1, WINDOW_A), bias_a,
                       _stack_rows(a_sinks[i], BAND_TQ), WINDOW_A)
        xp = _finish(xp, gt_p, z, [o], w_out)
        a_new_p.append(kv[:, seq - WINDOW_A:].reshape(n_p, WINDOW_A, 2, N_KV_HEADS, HEAD_DIM))
        q, kv, _, _, z = _mod_proj(xs, a_norm_g[i], sh_s, sc_s, ws, outs, 128)
        new = kv.reshape(n_s, 1, KV_ROW)
        o = _dec_attn(q.reshape(n_s, GROUP, LANES), _dim_token(state_a_kv[i]), new, dbias_a, dnew,
                      a_sinks[i].astype(F32).reshape(N_KV_HEADS, GROUP, 1))
        xs = _finish(xs, gt_s, z, [o.reshape(1, n_s, HD)], w_out)
        a_new_s.append(jnp.concatenate([state_a_kv[i][:, 1:], kv.reshape(n_s, 1, 2, N_KV_HEADS, HEAD_DIM)], axis=1))

    (sh_p, sc_p), (sh_s, sc_s) = mods(ada_kv[0], 2)
    w_kv = [kv_w_in.astype(BF16)]
    outs = [(0, KV_ROW * b, KV_ROW * (b + 1), F32) for b in range(N_BRANCH)]
    outs += [(0, KV_ROW + KVD * h, KV_ROW + KVD * (h + 1), BF16) for h in range(4)]
    cmp_p, sel_p, win_p, sel_kp, sel_vp, win_kp, win_vp = _mod_proj(xp, kv_norm_g, sh_p, sc_p, w_kv, outs, 512)
    cmp_s, sel_s, win_s = _mod_proj(xs, kv_norm_g, sh_s, sc_s, w_kv, outs[:N_BRANCH], 128)
    cmp_s, sel_s, win_s = (r.reshape(n_s, 1, KV_ROW) for r in (cmp_s, sel_s, win_s))

    eye_g = jnp.eye(N_KV_HEADS, dtype=F32)
    w1e = jnp.einsum('kbdh,gG->bkgdGh', cmp_w1, eye_g).reshape(
        HALF_ROWS, KVD, N_KV_HEADS * CMP_HIDDEN).astype(BF16)
    w2e = jnp.einsum('khd,gG->kghGd', cmp_w2, eye_g).reshape(2, N_KV_HEADS * CMP_HIDDEN, KVD).astype(BF16)
    pos_e = jnp.broadcast_to(cmp_pos[:, :, None, :], (CMP_BLOCK, 2, N_KV_HEADS, HEAD_DIM)).reshape(HALF_ROWS, KVD)
    comp_p = _compress(cmp_p, pos_e, w1e, w2e)
    comp_past = _compress_paged(_dim_token(cache_cmp_kv), page_table, pos_e, w1e, w2e)
    tail = jnp.pad(cmp_s, ((0, 0), (0, SEL_BLOCK - 1), (0, 0))).reshape(1, n_s * SEL_BLOCK, KV_ROW)
    comp_tail = _compress(tail, pos_e, w1e, w2e).reshape(n_s, SEL_BLOCK // CMP_BLOCK, KV_ROW)
    comp_s = jnp.concatenate([comp_past, comp_tail], axis=1)

    nsp_p = -(-(seq // SEL_BLOCK) // LANES) * LANES
    nsp_s = -(-(past // SEL_BLOCK + 1) // LANES) * LANES
    ck_p, cv_p = _even_odd(comp_p, nsp_p)
    ck_s, cv_s = _even_odd(comp_s, nsp_s)

    sel_vp0, sel_vp1 = (_front_pad(v, tq) for v in _with_ones(sel_vp))
    win_vp0, win_vp1 = (_front_pad(v, WINDOW_B) for v in _with_ones(win_vp))
    sel_kp, win_kp = _front_pad(sel_kp, tq), _front_pad(win_kp, WINDOW_B)
    key_block = (jnp.arange(tq + seq) - tq) // SEL_BLOCK
    et = ((key_block[:, None] == jnp.arange(nsp_p)[None, :]) & (jnp.arange(tq + seq) >= tq)[:, None]).astype(BF16)
    nb_sel = _near_bias(table, tq)
    bias_b = _band_bias(table, tq, WINDOW_B)
    dbias_b, _ = _decode_bias(table, WINDOW_B, WINDOW_B)
    win_buf_t = _dim_token(state_win_kv)
    kpos = jnp.arange(past).reshape(n_pages, PAGE_SIZE)
    bt = jnp.transpose(_bias_of(table, past - kpos), (0, 2, 1))
    sel_pool_t = _dim_token(cache_sel_kv)

    head_of = _PERM // HEAD_DIM
    e_np = np.zeros((LANES, N_BRANCH * HD), np.float32)
    for br in range(N_BRANCH):
        e_np[br * N_HEADS + head_of, br * HD + np.arange(HD)] = 1.0
    e_gl = jnp.asarray(e_np, dtype=BF16)

    y_p = y_s = None
    for j in range(n_b):
        w_in = b_w_in[j]
        w_z = _pair_cols(w_in[:, HD:HD + N_BRANCH * HD].reshape(d, N_BRANCH, HD)).reshape(d, N_BRANCH * HD)
        ws = [_pair_cols(w_in[:, :HD]).astype(BF16), w_z.astype(BF16),
              jnp.pad(w_in[:, HD + N_BRANCH * HD:], ((0, 0), (0, LANES - N_BRANCH * N_HEADS))).astype(BF16)]
        outs = [(0, 0, HD, BF16), (1, 0, N_BRANCH * HD, BF16), (2, 0, LANES, F32)]
        w_out = _pair_rows(b_w_out[j]).astype(BF16)
        fg = final_norm_g if j == n_b - 1 else None
        (sh_p, sc_p, gt_p), (sh_s, sc_s, gt_s) = mods(ada_b[j], 3)
        q, z, gl = _mod_proj(xp, b_norm_g[j], sh_p, sc_p, ws, outs, 256)
        o_cmp, sel = _cmp_topk(q, ck_p, cv_p)
        o_sel = _sel_attn(q, sel, sel_kp, sel_vp0, sel_vp1, et, nb_sel)
        o_win = _band_attn(q, win_kp, win_vp0, win_vp1, bias_b, None, WINDOW_B)
        r = _finish(xp, gt_p, z, [o_cmp, o_sel, o_win], w_out, gl, e_gl, fg, 256)
        xp, y_p = r if fg is not None else (r, None)
        q, z, gl = _mod_proj(xs, b_norm_g[j], sh_s, sc_s, ws, outs, 128)
        q = q.reshape(n_s, GROUP, LANES)
        o_cmp, idx = _dec_cmp_topk(q, ck_s, cv_s, past)
        idx = idx.reshape(n_s, N_KV_HEADS, LANES)[:, :, :SEL_TOPK].reshape(-1)
        o_sel = _dec_sel(idx, page_table, q, sel_s, bt, dnew, sel_pool_t)
        o_win = _dec_attn(q, win_buf_t, win_s, dbias_b, dnew, None)
        r = _finish(xs, gt_s, z, [o.reshape(1, n_s, HD) for o in (o_cmp, o_sel, o_win)], w_out, gl, e_gl, fg, 128)
        xs, y_s = r if fg is not None else (r, None)

    kv5 = lambda r: r.reshape(r.shape[0], r.shape[1], 2, N_KV_HEADS, HEAD_DIM)
    new_win_s = jnp.concatenate([state_win_kv[:, 1:], kv5(win_s)], axis=1)
    return (y_p, y_s.reshape(n_s, 1, d), jnp.stack(a_new_p), jnp.stack(a_new_s), kv5(cmp_p), kv5(cmp_s),
            kv5(sel_p), kv5(sel_s), kv5(win_p[:, seq - WINDOW_B:]), new_win_s)
```

```python
import functools
import math

import numpy as np
import jax
import jax.numpy as jnp
from jax import lax
from jax.experimental import pallas as pl
from jax.experimental.pallas import tpu as pltpu

F32, BF16 = jnp.float32, jnp.bfloat16

D_MODEL = 1024
N_HEADS = 16
HEAD_DIM = 64
N_KV_HEADS = 2
GROUP = N_HEADS // N_KV_HEADS
HD = N_HEADS * HEAD_DIM
KVD = N_KV_HEADS * HEAD_DIM
KV_ROW = 2 * KVD
WINDOW_A = 128
WINDOW_B = 512
Q_BLOCK = 128
CMP_BLOCK = 32
SEL_BLOCK = 64
SEL_TOPK = 16
CMP_HIDDEN = 2 * HEAD_DIM
N_BRANCH = 3
N_BUCKETS = 32
T5_MAX_DISTANCE = 128
RMS_EPS = 1e-6
ATTN_SCALE = HEAD_DIM ** -0.5
PAGE_SIZE = 128
LANES = 128
NEG = -1e30
VMEM_LIMIT = 56 * 1024 * 1024

_PERM = np.array([(GROUP * g + c) * HEAD_DIM + d for c in range(GROUP) for g in range(N_KV_HEADS)
                  for d in range(HEAD_DIM)], dtype=np.int32)


def _params(sem):
    return pltpu.CompilerParams(dimension_semantics=sem, vmem_limit_bytes=VMEM_LIMIT)


def _silu(x):
    return x * jax.nn.sigmoid(x)


def _ada_kernel(c_ref, w_ref, b_ref, o_ref):
    a = _silu(c_ref[...]).astype(BF16)
    o_ref[0] = jnp.dot(a, w_ref[0].astype(BF16), preferred_element_type=F32) + b_ref[0]


def _ada(c_all, w, b):
    nl, d, n = w.shape
    m = c_all.shape[0]
    tn = 512
    return pl.pallas_call(
        _ada_kernel,
        grid=(nl, n // tn),
        in_specs=[pl.BlockSpec((m, d), lambda l, j: (0, 0)),
                  pl.BlockSpec((1, d, tn), lambda l, j: (l, 0, j)),
                  pl.BlockSpec((1, 1, tn), lambda l, j: (l, 0, j))],
        out_specs=pl.BlockSpec((1, m, tn), lambda l, j: (l, 0, j)),
        out_shape=jax.ShapeDtypeStruct((nl, m, n), F32),
        compiler_params=_params(("parallel", "parallel")),
        name="ada_params",
    )(c_all, w, b.reshape(nl, 1, n))


def _mod_proj_kernel(x_ref, g_ref, sh_ref, sc_ref, *refs, nw, outs):
    w_refs, o_refs = refs[:nw], refs[nw:]
    x = x_ref[0]
    y = x * lax.rsqrt(jnp.mean(x * x, axis=-1, keepdims=True) + RMS_EPS) * g_ref[...]
    h = (y * (1.0 + sc_ref[0]) + sh_ref[0]).astype(BF16)
    res = {}
    for (wi, c0, c1, dt), o_ref in zip(outs, o_refs):
        if wi not in res:
            res[wi] = jnp.dot(h, w_refs[wi][...], preferred_element_type=F32)
        o_ref[0] = res[wi][:, c0:c1].astype(dt)


def _mod_proj(x, g, shift, scale, ws, outs, tm):
    nb, t, d = x.shape
    tm = min(tm, t)
    if shift.shape[1] == 1:
        mod_spec = pl.BlockSpec((1, 1, d), lambda n, i: (n, 0, 0))
    else:
        mod_spec = pl.BlockSpec((1, tm, d), lambda n, i: (n, i, 0))
    in_specs = [pl.BlockSpec((1, tm, d), lambda n, i: (n, i, 0)),
                pl.BlockSpec((1, d), lambda n, i: (0, 0)), mod_spec, mod_spec]
    in_specs += [pl.BlockSpec(w.shape, lambda n, i: (0, 0)) for w in ws]
    out_specs = [pl.BlockSpec((1, tm, c1 - c0), lambda n, i: (n, i, 0)) for (_, c0, c1, _) in outs]
    out_shape = [jax.ShapeDtypeStruct((nb, t, c1 - c0), dt) for (_, c0, c1, dt) in outs]
    return pl.pallas_call(
        functools.partial(_mod_proj_kernel, nw=len(ws), outs=tuple(outs)),
        grid=(nb, t // tm),
        in_specs=in_specs, out_specs=out_specs, out_shape=out_shape,
        compiler_params=_params(("parallel", "parallel")),
        name="mod_proj",
    )(x, g.reshape(1, d), shift, scale, *ws)


def _stack_queries(q_ref, qs_ref, tq):
    lane = lax.broadcasted_iota(jnp.int32, (tq, LANES), 1)
    for c in range(GROUP):
        qc = (q_ref[0, :, LANES * c:LANES * (c + 1)].astype(F32) * ATTN_SCALE)
        qs_ref[0, c * tq:(c + 1) * tq, :LANES] = jnp.where(lane < HEAD_DIM, qc, 0.0).astype(BF16)
        qs_ref[1, c * tq:(c + 1) * tq, :LANES] = jnp.where(lane >= HEAD_DIM, qc, 0.0).astype(BF16)


def _store_paired(o_ref, o0, o1, tq):
    lane = lax.broadcasted_iota(jnp.int32, o0.shape, 1)
    oc = jnp.where(lane < HEAD_DIM, o0, o1)
    for c in range(GROUP):
        o_ref[0, :, LANES * c:LANES * (c + 1)] = oc[c * tq:(c + 1) * tq].astype(o_ref.dtype)


def _nt_dot(a, b):
    return lax.dot_general(a, b, (((1,), (1,)), ((), ())), preferred_element_type=F32)


def _row_max(s):
    cm = s[:, :LANES]
    for c in range(1, s.shape[1] // LANES):
        cm = jnp.maximum(cm, s[:, LANES * c:LANES * (c + 1)])
    return jnp.max(cm, axis=-1, keepdims=True)


def _lanes(m, width):
    return m if width == LANES else jnp.concatenate([m] * (width // LANES), axis=1)


def _normalize(acc):
    return acc / pltpu.roll(acc, HEAD_DIM, axis=1)


def _with_ones(v):
    lane = lax.broadcasted_iota(jnp.int32, v.shape, v.ndim - 1)
    one = jnp.ones_like(v)
    return jnp.where(lane < HEAD_DIM, v, one), jnp.where(lane >= HEAD_DIM, v, one)


BAND_TQ = 256

def _band_attn_kernel(*refs, tq, window, use_sink):
    if use_sink:
        q_ref, k_ref, v0_ref, v1_ref, bias_ref, sink_ref, o_ref, qs_ref = refs
    else:
        q_ref, k_ref, v0_ref, v1_ref, bias_ref, o_ref, qs_ref = refs
    s0 = pl.multiple_of(pl.program_id(1) * tq, tq)
    band = window + tq
    rows = GROUP * tq
    _stack_queries(q_ref, qs_ref, tq)
    kb = k_ref[0, pl.ds(s0, band), :]
    kpos = s0 - window + lax.broadcasted_iota(jnp.int32, (1, band), 1)
    front = jnp.where(kpos >= 0, 0.0, NEG)
    lane = lax.broadcasted_iota(jnp.int32, (rows, LANES), 1)
    outs = []
    for g, v_ref in enumerate((v0_ref, v1_ref)):
        s = _nt_dot(qs_ref[g], kb) + bias_ref[g] + front
        m = jnp.broadcast_to(_row_max(s), (rows, LANES))
        if use_sink:
            m = jnp.maximum(m, sink_ref[g])
        e = jnp.exp(s - _lanes(m, band))
        acc = jnp.dot(e.astype(BF16), v_ref[0, pl.ds(s0, band), :], preferred_element_type=F32)
        if use_sink:
            acc = acc + jnp.where((lane < HEAD_DIM) == (g == 0), 0.0, jnp.exp(sink_ref[g] - m))
        outs.append(_normalize(acc))
    _store_paired(o_ref, outs[0], outs[1], tq)


def _band_attn(q, kp, vp0, vp1, bias, sink, window):
    n, s, _ = q.shape
    tq = bias.shape[1] // GROUP
    band = window + tq
    use_sink = sink is not None
    once = pl.Buffered(1)
    kv_spec = pl.BlockSpec((1, window + s, KVD), lambda b, i: (b, 0, 0))
    in_specs = [pl.BlockSpec((1, tq, HD), lambda b, i: (b, i, 0)), kv_spec, kv_spec, kv_spec,
                pl.BlockSpec((N_KV_HEADS, GROUP * tq, band), lambda b, i: (0, 0, 0), pipeline_mode=once)]
    args = [q, kp, vp0, vp1, bias]
    if use_sink:
        in_specs.append(pl.BlockSpec((N_KV_HEADS, GROUP * tq, LANES), lambda b, i: (0, 0, 0), pipeline_mode=once))
        args.append(sink)
    return pl.pallas_call(
        functools.partial(_band_attn_kernel, tq=tq, window=window, use_sink=use_sink),
        grid=(n, s // tq),
        in_specs=in_specs,
        out_specs=pl.BlockSpec((1, tq, HD), lambda b, i: (b, i, 0)),
        out_shape=jax.ShapeDtypeStruct((n, s, HD), BF16),
        scratch_shapes=[pltpu.VMEM((N_KV_HEADS, GROUP * tq, LANES), BF16)],
        compiler_params=_params(("parallel", "arbitrary")),
        name="band_attn",
    )(*args)


def _cmp_softmax(s3, negm, has_valid):
    def fold(x, op):
        acc = x[..., :LANES]
        for c in range(1, x.shape[-1] // LANES):
            acc = op(acc, x[..., LANES * c:LANES * (c + 1)])
        return acc

    s3 = s3 + negm
    m = jnp.max(fold(s3, jnp.maximum), axis=-1, keepdims=True)
    m = jnp.where(has_valid, m, 0.0)
    e = jnp.exp(s3 - m)
    den = jnp.sum(fold(e, jnp.add), axis=-1, keepdims=True)
    return e / jnp.where(den > 0, den, 1.0)


def _cmp_mask(t, nsp):
    u = lax.broadcasted_iota(jnp.int32, (t.shape[0], 2 * nsp), 1)
    cend = jnp.where(u < nsp, SEL_BLOCK * u + (CMP_BLOCK - 1), SEL_BLOCK * (u - nsp) + (SEL_BLOCK - 1))
    return jnp.where(cend <= t, 0.0, NEG), t >= CMP_BLOCK - 1


def _select_blocks(imp, t):
    rows, nsp = imp.shape
    j = lax.broadcasted_iota(jnp.int32, (rows, nsp), 1)
    jf = j.astype(F32)
    cur = t // SEL_BLOCK
    forced = (j == 0) | (j == cur) | (j == cur - 1)
    work = jnp.where(forced, jnp.inf, jnp.where(j <= cur, imp, -jnp.inf))
    sel = jnp.zeros((rows, nsp), F32)
    lane = lax.broadcasted_iota(jnp.int32, (rows, LANES), 1)
    idx = jnp.full((rows, LANES), -1.0, F32)
    for k in range(SEL_TOPK):
        m = jnp.max(work, axis=-1, keepdims=True)
        first = jnp.min(jnp.where(work == m, jf, float(nsp)), axis=-1, keepdims=True)
        hit = jf == first
        ok = m > -jnp.inf
        sel = jnp.where(hit & ok, 1.0, sel)
        idx = jnp.where(lane == k, jnp.where(ok, first, -1.0), idx)
        work = jnp.where(hit, -jnp.inf, work)
    return sel, idx


def _select_blocks_t(imp_t, t):
    nsp, cols = imp_t.shape
    j = lax.broadcasted_iota(jnp.int32, (nsp, cols), 0)
    jf = j.astype(F32)
    cur = t // SEL_BLOCK
    forced = (j == 0) | (j == cur) | (j == cur - 1)
    work = jnp.where(forced, jnp.inf, jnp.where(j <= cur, imp_t, -jnp.inf))
    sel = jnp.zeros((nsp, cols), F32)
    for _ in range(SEL_TOPK):
        m = jnp.max(work, axis=0, keepdims=True)
        first = jnp.min(jnp.where(work == m, jf, float(nsp)), axis=0, keepdims=True)
        hit = jf == first
        sel = jnp.where(hit & (m > -jnp.inf), 1.0, sel)
        work = jnp.where(hit, -jnp.inf, work)
    return sel


def _cmp_topk_kernel(q_ref, ck_ref, cv_ref, o_ref, sel_ref, qs_ref, *, tq, nsp):
    s0 = pl.program_id(1) * tq
    _stack_queries(q_ref, qs_ref, tq)
    t = s0 + lax.broadcasted_iota(jnp.int32, (tq, 1), 0)
    t_row = s0 + lax.broadcasted_iota(jnp.int32, (1, tq), 1)
    negm, has_valid = _cmp_mask(t, nsp)
    ck, cv = ck_ref[0], cv_ref[0]
    outs = []
    for g in range(N_KV_HEADS):
        s3 = _nt_dot(qs_ref[g], ck).reshape(GROUP, tq, 2 * nsp)
        p = _cmp_softmax(s3, negm[None], has_valid[None])
        outs.append(jnp.dot(p.reshape(GROUP * tq, 2 * nsp).astype(BF16), cv, preferred_element_type=F32))
        ph = jnp.sum(p, axis=0)
        sel_t = _select_blocks_t((ph[:, :nsp] + ph[:, nsp:]).T, t_row)
        sel_ref[0, g] = sel_t.T.astype(sel_ref.dtype)
    _store_paired(o_ref, outs[0], outs[1], tq)


def _cmp_topk(q, ck, cv):
    n, s, _ = q.shape
    tq = Q_BLOCK
    nsp = ck.shape[1] // 2
    return pl.pallas_call(
        functools.partial(_cmp_topk_kernel, tq=tq, nsp=nsp),
        grid=(n, s // tq),
        in_specs=[pl.BlockSpec((1, tq, HD), lambda b, i: (b, i, 0)),
                  pl.BlockSpec((1, 2 * nsp, KVD), lambda b, i: (b, 0, 0)),
                  pl.BlockSpec((1, 2 * nsp, KVD), lambda b, i: (b, 0, 0))],
        out_specs=[pl.BlockSpec((1, tq, HD), lambda b, i: (b, i, 0)),
                   pl.BlockSpec((1, N_KV_HEADS, tq, nsp), lambda b, i: (b, 0, i, 0))],
        out_shape=[jax.ShapeDtypeStruct((n, s, HD), BF16),
                   jax.ShapeDtypeStruct((n, N_KV_HEADS, s, nsp), BF16)],
        scratch_shapes=[pltpu.VMEM((N_KV_HEADS, GROUP * tq, LANES), BF16)],
        compiler_params=_params(("parallel", "arbitrary")),
        name="cmp_topk",
    )(q, ck, cv)


FAR_TILES = 4


def _sel_attn_kernel(q_ref, sel_ref, k_ref, v0_ref, v1_ref, et_ref, nb_ref, o_ref, qs_ref, m_ref, acc_ref, *, tq):
    qt = pl.program_id(1)
    s0 = pl.multiple_of(qt * tq, tq)
    rows = GROUP * tq
    v_refs = (v0_ref, v1_ref)
    _stack_queries(q_ref, qs_ref, tq)
    for g in range(N_KV_HEADS):
        unselected = 1.0 - sel_ref[0, g]
        for c in range(GROUP):
            qs_ref[g, c * tq:(c + 1) * tq, LANES:] = unselected
    kpos = s0 - tq + lax.broadcasted_iota(jnp.int32, (1, 2 * tq), 1)
    front = jnp.where(kpos >= 0, 0.0, NEG)

    def key_tile(g, off, width, bias, first):
        kf = jnp.concatenate([k_ref[0, pl.ds(off, width), :], et_ref[pl.ds(off, width), :]], axis=1)
        vf = v_refs[g][0, pl.ds(off, width), :]
        s = _nt_dot(qs_ref[g], kf)
        if bias is not None:
            lead = width - 2 * tq
            near = s[:, lead:].reshape(GROUP, tq, 2 * tq) + bias.reshape(GROUP, tq, 2 * tq)
            near = near.reshape(rows, 2 * tq) + front
            s = near if lead == 0 else jnp.concatenate([s[:, :lead], near], axis=1)
        rm = _row_max(s)
        if first:
            m_new = jnp.broadcast_to(rm, (rows, LANES))
        else:
            m_old = m_ref[g]
            m_new = jnp.maximum(m_old, rm)
        pv = jnp.dot(jnp.exp(s - _lanes(m_new, width)).astype(BF16), vf, preferred_element_type=F32)
        acc_ref[g] = pv if first else jnp.exp(m_old - m_new) * acc_ref[g] + pv
        m_ref[g] = m_new

    n_far = jnp.maximum(qt - 1, 0)
    n_wide = n_far // FAR_TILES
    n_rest = n_far - n_wide * FAR_TILES

    for rest in range(FAR_TILES):
        @pl.when(n_rest == rest)
        def _():
            off = pl.multiple_of(s0 - rest * tq, tq)
            for g in range(N_KV_HEADS):
                key_tile(g, off, (rest + 2) * tq, nb_ref[g], True)

    def wide(j, carry):
        off = pl.multiple_of(tq + j * (FAR_TILES * tq), tq)
        for g in range(N_KV_HEADS):
            key_tile(g, off, FAR_TILES * tq, None, False)
        return carry

    lax.fori_loop(0, n_wide, wide, 0)
    _store_paired(o_ref, _normalize(acc_ref[0]), _normalize(acc_ref[1]), tq)


def _sel_attn(q, sel, kp, vp0, vp1, et, nb):
    n, s, _ = q.shape
    tq = Q_BLOCK
    nsp = sel.shape[-1]
    assert nsp == LANES
    kv_spec = pl.BlockSpec((1, tq + s, KVD), lambda b, i: (b, 0, 0))
    return pl.pallas_call(
        functools.partial(_sel_attn_kernel, tq=tq),
        grid=(n, s // tq),
        in_specs=[pl.BlockSpec((1, tq, HD), lambda b, i: (b, i, 0)),
                  pl.BlockSpec((1, N_KV_HEADS, tq, nsp), lambda b, i: (b, 0, i, 0)),
                  kv_spec, kv_spec, kv_spec,
                  pl.BlockSpec((tq + s, nsp), lambda b, i: (0, 0)),
                  pl.BlockSpec((N_KV_HEADS, GROUP * tq, 2 * tq), lambda b, i: (0, 0, 0))],
        out_specs=pl.BlockSpec((1, tq, HD), lambda b, i: (b, i, 0)),
        out_shape=jax.ShapeDtypeStruct((n, s, HD), BF16),
        scratch_shapes=[pltpu.VMEM((N_KV_HEADS, GROUP * tq, 2 * LANES), BF16),
                        pltpu.VMEM((N_KV_HEADS, GROUP * tq, LANES), F32),
                        pltpu.VMEM((N_KV_HEADS, GROUP * tq, LANES), F32)],
        compiler_params=_params(("parallel", "arbitrary")),
        name="sel_attn",
    )(q, sel, kp, vp0, vp1, et, nb)


def _finish_kernel(*refs, n_branch, final):
    x_ref, gate_ref, z_ref = refs[:3]
    pos = 3
    if n_branch > 1:
        gl_ref, e_ref = refs[pos:pos + 2]
        pos += 2
    o_refs = refs[pos:pos + n_branch]
    pos += n_branch
    w_ref = refs[pos]
    pos += 1
    if final:
        fg_ref = refs[pos]
        pos += 1
    out_ref = refs[pos]
    mix = None
    if n_branch > 1:
        sg = jax.nn.sigmoid(gl_ref[0]).astype(BF16)
    for br in range(n_branch):
        t = o_refs[br][0].astype(F32) * _silu(z_ref[0, :, HD * br:HD * (br + 1)].astype(F32))
        if n_branch > 1:
            t = t * jnp.dot(sg, e_ref[:, HD * br:HD * (br + 1)], preferred_element_type=F32)
        mix = t if mix is None else mix + t
    upd = jnp.dot(mix.astype(BF16), w_ref[...], preferred_element_type=F32)
    xn = x_ref[0] + gate_ref[0] * upd
    out_ref[0] = xn
    if final:
        y = xn * lax.rsqrt(jnp.mean(xn * xn, axis=-1, keepdims=True) + RMS_EPS) * fg_ref[...]
        refs[pos + 1][0] = y


def _finish(x, gate, z, os_, w, gl=None, e=None, final_g=None, tm=512):
    nb, t, d = x.shape
    tm = min(tm, t)
    n_branch = len(os_)
    final = final_g is not None
    row = lambda c: pl.BlockSpec((1, tm, c), lambda n, i: (n, i, 0))
    if gate.shape[1] == 1:
        gate_spec = pl.BlockSpec((1, 1, d), lambda n, i: (n, 0, 0))
    else:
        gate_spec = row(d)
    in_specs = [row(d), gate_spec, row(z.shape[-1])]
    args = [x, gate, z]
    if n_branch > 1:
        in_specs += [row(gl.shape[-1]), pl.BlockSpec(e.shape, lambda n, i: (0, 0))]
        args += [gl, e]
    in_specs += [row(HD)] * n_branch + [pl.BlockSpec(w.shape, lambda n, i: (0, 0))]
    args += list(os_) + [w]
    out_specs, out_shape = [row(d)], [jax.ShapeDtypeStruct((nb, t, d), F32)]
    if final:
        in_specs.append(pl.BlockSpec((1, d), lambda n, i: (0, 0)))
        args.append(final_g.reshape(1, d))
        out_specs.append(row(d))
        out_shape.append(jax.ShapeDtypeStruct((nb, t, d), F32))
    res = pl.pallas_call(
        functools.partial(_finish_kernel, n_branch=n_branch, final=final),
        grid=(nb, t // tm),
        in_specs=in_specs, out_specs=out_specs, out_shape=out_shape,
        compiler_params=_params(("parallel", "parallel")),
        name="finish",
    )(*args)
    return res if final else res[0]


HALF_ROWS = 2 * CMP_BLOCK


def _compress_rows(xb, pos_ref, w1_ref, w2_ref, side=None):
    outs = []
    for k in range(2):
        hid = None
        for b in range(CMP_BLOCK):
            r = 2 * b + k
            xv = (xb(r) + pos_ref[r:r + 1, :]).astype(BF16)
            part = jnp.dot(xv, w1_ref[r], preferred_element_type=F32)
            hid = part if hid is None else hid + part
            if side is not None:
                side(k * CMP_BLOCK + b)
        outs.append(jnp.dot(_silu(hid).astype(BF16), w2_ref[k], preferred_element_type=F32))
    return jnp.concatenate(outs, axis=1)


def _compress_kernel(x_ref, pos_ref, w1_ref, w2_ref, o_ref, *, nblk):
    o_ref[0] = _compress_rows(lambda r: x_ref[0, pl.ds(r, nblk, stride=HALF_ROWS), :], pos_ref, w1_ref, w2_ref)


def _compress(rows, pos_e, w1e, w2e):
    nb, t, _ = rows.shape
    nblk = t // CMP_BLOCK
    return pl.pallas_call(
        functools.partial(_compress_kernel, nblk=nblk),
        grid=(nb,),
        in_specs=[pl.BlockSpec((1, 2 * t, KVD), lambda n: (n, 0, 0)),
                  pl.BlockSpec(pos_e.shape, lambda n: (0, 0)),
                  pl.BlockSpec(w1e.shape, lambda n: (0, 0, 0)),
                  pl.BlockSpec(w2e.shape, lambda n: (0, 0, 0))],
        out_specs=pl.BlockSpec((1, nblk, KV_ROW), lambda n: (n, 0, 0)),
        out_shape=jax.ShapeDtypeStruct((nb, nblk, KV_ROW), F32),
        compiler_params=_params(("parallel",)),
        name="compress",
    )(rows.reshape(nb, 2 * t, KVD), pos_e, w1e, w2e)


def _page_copies(pt_ref, pool_ref, raw_ref, sem_ref, n, slot, n_pages):
    return [pltpu.make_async_copy(pool_ref.at[pt_ref[n * n_pages + p]], raw_ref.at[slot, p], sem_ref.at[slot, p])
            for p in range(n_pages)]


CMP_PITCH = CMP_BLOCK + 2
BLOCKS_PER_CMP_PAGE = PAGE_SIZE // CMP_BLOCK


def _compress_paged_kernel(pt_ref, pool_ref, pos_ref, w1_ref, w2_ref, o_ref, raw_ref, x_ref, sem_ref, *,
                           n_pages, n_samples):
    n = pl.program_id(0)
    cur = n % 2
    nxt = 1 - cur
    copies = lambda sample, slot: _page_copies(pt_ref, pool_ref, raw_ref, sem_ref, sample, slot, n_pages)

    def to_token_rows(slot, p):
        for k in range(2):
            t = raw_ref[slot, p, KVD * k:KVD * (k + 1), :].T
            for c in range(BLOCKS_PER_CMP_PAGE):
                row0 = (p * BLOCKS_PER_CMP_PAGE + c) * CMP_PITCH
                x_ref[slot, k, pl.ds(row0, CMP_BLOCK), :] = t[c * CMP_BLOCK:(c + 1) * CMP_BLOCK]

    @pl.when(n == 0)
    def _():
        for cp in copies(0, 0):
            cp.start()
        if n_samples > 1:
            for cp in copies(1, 1):
                cp.start()
        for cp in copies(0, 0):
            cp.wait()

        def first(p, carry):
            to_token_rows(0, p)
            return carry

        lax.fori_loop(0, n_pages, first, 0)

    @pl.when(n + 1 < n_samples)
    def _():
        for cp in copies(n + 1, nxt):
            cp.wait()

    @pl.when(n + 2 < n_samples)
    def _():
        for cp in copies(n + 2, cur):
            cp.start()

    per_call = -(-n_pages // HALF_ROWS)

    def side(i):
        for p in range(i * per_call, min((i + 1) * per_call, n_pages)):
            to_token_rows(nxt, p)

    half = n_pages * BLOCKS_PER_CMP_PAGE // 2

    def row_of_blocks(r):
        k, b = r % 2, r // 2
        return jnp.concatenate([x_ref[cur, k, pl.ds(b, half, stride=2 * CMP_PITCH), :],
                                x_ref[cur, k, pl.ds(CMP_PITCH + b, half, stride=2 * CMP_PITCH), :]], axis=0)

    o_ref[0] = _compress_rows(row_of_blocks, pos_ref, w1_ref, w2_ref, side if n_samples > 1 else None)


def _compress_paged(pool_t, page_table, pos_e, w1e, w2e):
    ns, n_pages = page_table.shape
    past = n_pages * PAGE_SIZE
    nblk = past // CMP_BLOCK
    grid_spec = pltpu.PrefetchScalarGridSpec(
        num_scalar_prefetch=1,
        grid=(ns,),
        in_specs=[pl.BlockSpec(memory_space=pl.ANY),
                  pl.BlockSpec(pos_e.shape, lambda n, pt: (0, 0)),
                  pl.BlockSpec(w1e.shape, lambda n, pt: (0, 0, 0)),
                  pl.BlockSpec(w2e.shape, lambda n, pt: (0, 0, 0))],
        out_specs=pl.BlockSpec((1, nblk, KV_ROW), lambda n, pt: (n, 0, 0)),
        scratch_shapes=[pltpu.VMEM((2, n_pages, KV_ROW, PAGE_SIZE), F32),
                        pltpu.VMEM((2, 2, nblk * CMP_PITCH, KVD), F32),
                        pltpu.SemaphoreType.DMA((2, n_pages))])
    return pl.pallas_call(
        functools.partial(_compress_paged_kernel, n_pages=n_pages, n_samples=ns),
        grid_spec=grid_spec,
        out_shape=jax.ShapeDtypeStruct((ns, nblk, KV_ROW), F32),
        compiler_params=_params(("arbitrary",)),
        name="compress_paged",
    )(page_table.reshape(-1), pool_t, pos_e, w1e, w2e)


def _decode_queries(q):
    lane = lax.broadcasted_iota(jnp.int32, q.shape, 1)
    qs = q * ATTN_SCALE
    return [jnp.where(lane < HEAD_DIM, qs, 0.0), jnp.where(lane >= HEAD_DIM, qs, 0.0)]


def _decode_group(qg, kts, vts, biases, new_row, bias_new, new_on, sink):
    qb = qg.astype(BF16)
    ss = [jnp.dot(qb, kt, preferred_element_type=F32) + b for kt, b in zip(kts, biases)]
    s_new = jnp.sum(qg * new_row[:, :KVD], axis=-1, keepdims=True) + bias_new
    if new_on is not None:
        s_new = jnp.where(new_on, s_new, NEG)
    m = s_new
    for s in ss:
        m = jnp.maximum(m, jnp.max(s, axis=-1, keepdims=True))
    if sink is not None:
        m = jnp.maximum(m, sink)
    e_new = jnp.exp(s_new - m)
    den = e_new
    if sink is not None:
        den = den + jnp.exp(sink - m)
    o = e_new * new_row[:, KVD:]
    for s, vt in zip(ss, vts):
        e = jnp.exp(s - m)
        den = den + jnp.sum(e, axis=-1, keepdims=True)
        o = o + _nt_dot(e.astype(BF16), vt)
    return o / den


def _pair_values(o0, o1):
    lane = lax.broadcasted_iota(jnp.int32, (GROUP, LANES), 1)
    return jnp.where(lane < HEAD_DIM, o0, o1)


def _dec_attn_kernel(*refs, nb, use_sink):
    if use_sink:
        q_ref, kv_ref, new_ref, bias_ref, bnew_ref, sink_ref, o_ref = refs
    else:
        q_ref, kv_ref, new_ref, bias_ref, bnew_ref, o_ref = refs
    for b in range(nb):
        qgs = _decode_queries(q_ref[b].astype(F32))
        kt = kv_ref[b, :KVD, :].astype(BF16)
        vt = kv_ref[b, KVD:, :].astype(BF16)
        outs = [_decode_group(qgs[g], [kt], [vt], [bias_ref[g]], new_ref[b], bnew_ref[g], None,
                              sink_ref[g] if use_sink else None) for g in range(N_KV_HEADS)]
        o_ref[b] = _pair_values(outs[0], outs[1]).astype(o_ref.dtype)


def _dec_attn(q, buf_t, new, bias, bias_new, sink):
    ns, _, l = buf_t.shape
    nb = 8
    use_sink = sink is not None
    const = lambda a: pl.BlockSpec(a.shape, lambda i: (0,) * a.ndim)
    in_specs = [pl.BlockSpec((nb, GROUP, LANES), lambda i: (i, 0, 0)),
                pl.BlockSpec((nb, KV_ROW, l), lambda i: (i, 0, 0)),
                pl.BlockSpec((nb, 1, KV_ROW), lambda i: (i, 0, 0)),
                const(bias), const(bias_new)]
    args = [q, buf_t, new, bias, bias_new]
    if use_sink:
        in_specs.append(const(sink))
        args.append(sink)
    return pl.pallas_call(
        functools.partial(_dec_attn_kernel, nb=nb, use_sink=use_sink),
        grid=(ns // nb,),
        in_specs=in_specs,
        out_specs=pl.BlockSpec((nb, GROUP, LANES), lambda i: (i, 0, 0)),
        out_shape=jax.ShapeDtypeStruct((ns, GROUP, LANES), BF16),
        compiler_params=_params(("parallel",)),
        name="dec_attn",
    )(*args)


def _dec_cmp_topk_kernel(q_ref, ck_ref, cv_ref, o_ref, idx_ref, imp_ref, *, nb, nsp, qpos):
    t1 = jnp.full((1, 1), qpos, jnp.int32)
    negm, has_valid = _cmp_mask(t1, nsp)
    lane = lax.broadcasted_iota(jnp.int32, (GROUP, LANES), 1)
    for b in range(nb):
        q = q_ref[b].astype(F32) * ATTN_SCALE
        outs = []
        for g in range(N_KV_HEADS):
            qg = jnp.where((lane < HEAD_DIM) == (g == 0), q, 0.0).astype(BF16)
            p = _cmp_softmax(_nt_dot(qg, ck_ref[b]), negm, has_valid)
            outs.append(jnp.dot(p.astype(BF16), cv_ref[b], preferred_element_type=F32))
            ph = jnp.sum(p, axis=0, keepdims=True)
            imp_ref[N_KV_HEADS * b + g:N_KV_HEADS * b + g + 1, :] = ph[:, :nsp] + ph[:, nsp:]
        o_ref[b] = jnp.where(lane < HEAD_DIM, outs[0], outs[1]).astype(o_ref.dtype)
    rows = N_KV_HEADS * nb
    _, idx = _select_blocks(imp_ref[...], jnp.full((rows, 1), qpos, jnp.int32))
    idx_ref[0] = idx.astype(jnp.int32)


def _dec_cmp_topk(q, ck, cv, qpos):
    ns = q.shape[0]
    nb = 8
    nsp = ck.shape[1] // 2
    return pl.pallas_call(
        functools.partial(_dec_cmp_topk_kernel, nb=nb, nsp=nsp, qpos=qpos),
        grid=(ns // nb,),
        in_specs=[pl.BlockSpec((nb, GROUP, LANES), lambda i: (i, 0, 0)),
                  pl.BlockSpec((nb, 2 * nsp, KVD), lambda i: (i, 0, 0)),
                  pl.BlockSpec((nb, 2 * nsp, KVD), lambda i: (i, 0, 0))],
        out_specs=[pl.BlockSpec((nb, GROUP, LANES), lambda i: (i, 0, 0)),
                   pl.BlockSpec((1, N_KV_HEADS * nb, LANES), lambda i: (i, 0, 0))],
        out_shape=[jax.ShapeDtypeStruct((ns, GROUP, LANES), BF16),
                   jax.ShapeDtypeStruct((ns // nb, N_KV_HEADS * nb, LANES), jnp.int32)],
        scratch_shapes=[pltpu.VMEM((N_KV_HEADS * nb, nsp), F32)],
        compiler_params=_params(("parallel",)),
        name="dec_cmp_topk",
    )(q, ck, cv)


BLOCKS_PER_PAGE = PAGE_SIZE // SEL_BLOCK


def _sel_pick(idx_ref, n, g, k, n_past):
    blk = idx_ref[(n * N_KV_HEADS + g) * SEL_TOPK + k]
    return jnp.clip(blk, 0, n_past - 1), (blk >= 0) & (blk < n_past), blk == n_past


def _sel_copies(idx_ref, pt_ref, pool_ref, buf_ref, sem_ref, n, slot, n_pages):
    copies = []
    for g in range(N_KV_HEADS):
        for k in range(SEL_TOPK):
            blk, _, _ = _sel_pick(idx_ref, n, g, k, n_pages * BLOCKS_PER_PAGE)
            phys = pt_ref[n * n_pages + blk // BLOCKS_PER_PAGE]
            copies.append(pltpu.make_async_copy(pool_ref.at[phys], buf_ref.at[slot, g, k], sem_ref.at[slot, g, k]))
    return copies


def _dec_sel_kernel(idx_ref, pt_ref, q_ref, new_ref, bt_ref, bnew_ref, pool_ref, o_ref, buf_ref, sem_ref, *,
                    n_pages):
    n = pl.program_id(0)
    slot = n % 2
    n_past = n_pages * BLOCKS_PER_PAGE

    @pl.when(n == 0)
    def _():
        for cp in _sel_copies(idx_ref, pt_ref, pool_ref, buf_ref, sem_ref, 0, 0, n_pages):
            cp.start()

    @pl.when(n + 1 < pl.num_programs(0))
    def _():
        for cp in _sel_copies(idx_ref, pt_ref, pool_ref, buf_ref, sem_ref, n + 1, 1 - slot, n_pages):
            cp.start()

    copies = _sel_copies(idx_ref, pt_ref, pool_ref, buf_ref, sem_ref, n, slot, n_pages)
    lane_block = lax.broadcasted_iota(jnp.int32, (GROUP, LANES), 1) // SEL_BLOCK
    biases, new_on = [], []
    for g in range(N_KV_HEADS):
        has_new = False
        rows = []
        for k in range(SEL_TOPK):
            blk, cached, is_new = _sel_pick(idx_ref, n, g, k, n_past)
            want = jnp.where(cached, blk % BLOCKS_PER_PAGE, BLOCKS_PER_PAGE)
            row = bt_ref[blk // BLOCKS_PER_PAGE, GROUP * g:GROUP * (g + 1), :]
            rows.append(jnp.where(lane_block == want, row, NEG))
            has_new = has_new | is_new
        biases.append(rows)
        new_on.append(has_new)
    for cp in copies:
        cp.wait()
    qgs = _decode_queries(q_ref[0].astype(F32))
    outs = []
    for g in range(N_KV_HEADS):
        kts = [buf_ref[slot, g, k, :KVD, :].astype(BF16) for k in range(SEL_TOPK)]
        vts = [buf_ref[slot, g, k, KVD:, :].astype(BF16) for k in range(SEL_TOPK)]
        outs.append(_decode_group(qgs[g], kts, vts, biases[g], new_ref[0], bnew_ref[g], new_on[g], None))
    o_ref[0] = _pair_values(outs[0], outs[1]).astype(o_ref.dtype)


def _dec_sel(idx, page_table, q, new, bt, bias_new, pool_t):
    ns, n_pages = page_table.shape
    grid_spec = pltpu.PrefetchScalarGridSpec(
        num_scalar_prefetch=2,
        grid=(ns,),
        in_specs=[pl.BlockSpec((1, GROUP, LANES), lambda n, ix, pt: (n, 0, 0)),
                  pl.BlockSpec((1, 1, KV_ROW), lambda n, ix, pt: (n, 0, 0)),
                  pl.BlockSpec(bt.shape, lambda n, ix, pt: (0, 0, 0)),
                  pl.BlockSpec(bias_new.shape, lambda n, ix, pt: (0, 0, 0)),
                  pl.BlockSpec(memory_space=pl.ANY)],
        out_specs=pl.BlockSpec((1, GROUP, LANES), lambda n, ix, pt: (n, 0, 0)),
        scratch_shapes=[pltpu.VMEM((2, N_KV_HEADS, SEL_TOPK, KV_ROW, PAGE_SIZE), F32),
                        pltpu.SemaphoreType.DMA((2, N_KV_HEADS, SEL_TOPK))])
    return pl.pallas_call(
        functools.partial(_dec_sel_kernel, n_pages=n_pages),
        grid_spec=grid_spec,
        out_shape=jax.ShapeDtypeStruct((ns, GROUP, LANES), BF16),
        compiler_params=_params(("arbitrary",)),
        name="dec_sel",
    )(idx, page_table.reshape(-1), q, new, bt, bias_new, pool_t)


def _t5_bucket(dist):
    n = np.maximum(dist, 0)
    exact = N_BUCKETS // 2
    nf = np.maximum(n, exact).astype(np.float32)
    scaled = np.log(nf / np.float32(exact)) / np.float32(math.log(T5_MAX_DISTANCE / exact)) * np.float32(N_BUCKETS - exact)
    large = exact + scaled.astype(np.int32)
    return np.where(n < exact, n, np.minimum(large, N_BUCKETS - 1)).astype(np.int32)


def _bias_of(table, dist):
    onehot = (jnp.asarray(_t5_bucket(dist))[..., None] == jnp.arange(N_BUCKETS)).astype(F32)
    return jnp.einsum('...b,bh->...h', onehot, table.astype(F32), precision=lax.Precision.HIGHEST)


def _stack_heads(b):
    return b.reshape(N_KV_HEADS, GROUP * b.shape[1], b.shape[2])


def _band_bias(table, tq, window):
    dist = np.arange(tq)[:, None] - np.arange(window + tq)[None, :] + window
    b = jnp.transpose(_bias_of(table, dist), (2, 0, 1))
    return _stack_heads(jnp.where(((dist >= 0) & (dist <= window))[None], b, NEG))


def _near_bias(table, tq):
    assert tq + 1 >= T5_MAX_DISTANCE
    dist = np.arange(tq)[:, None] - np.arange(2 * tq)[None, :] + tq
    tab = table.astype(F32)
    b = jnp.transpose(_bias_of(table, dist) - tab[N_BUCKETS - 1], (2, 0, 1))
    return _stack_heads(jnp.where((dist >= 0)[None], b, NEG))


def _decode_bias(table, length, window):
    dist = length - np.arange(length)
    tab = table.astype(F32)
    b = jnp.where((dist <= window)[None], _bias_of(table, dist).T, NEG)
    return b.reshape(N_KV_HEADS, GROUP, length), tab[0].reshape(N_KV_HEADS, GROUP, 1)


def _stack_rows(v, tq):
    return jnp.broadcast_to(v.astype(F32).reshape(N_KV_HEADS, GROUP, 1, 1),
                            (N_KV_HEADS, GROUP, tq, LANES)).reshape(N_KV_HEADS, GROUP * tq, LANES)


def _pair_cols(w):
    lead = w.shape[:-1]
    w = w.reshape(lead + (N_KV_HEADS, GROUP, HEAD_DIM))
    return jnp.swapaxes(w, -3, -2).reshape(lead + (HD,))


def _pair_rows(w):
    w = w.reshape(N_KV_HEADS, GROUP, HEAD_DIM, w.shape[-1])
    return jnp.swapaxes(w, 0, 1).reshape(HD, w.shape[-1])


def _dim_token(x):
    nd = x.ndim
    x = jnp.transpose(x, tuple(range(nd - 4)) + (nd - 3, nd - 2, nd - 1, nd - 4))
    return x.reshape(x.shape[:nd - 4] + (KV_ROW, x.shape[-1]))


def _even_odd(comp, nsp):
    def half(x):
        return jnp.pad(x, ((0, 0), (0, nsp - x.shape[1]), (0, 0)))
    eo = jnp.concatenate([half(comp[:, 0::2]), half(comp[:, 1::2])], axis=1).astype(BF16)
    return eo[:, :, :KVD], eo[:, :, KVD:]


def _front_pad(x, rows):
    return jnp.pad(x, ((0, 0), (rows, 0), (0, 0)))


def kernel(x_prompt, x_sample, state_a_kv, cache_cmp_kv, cache_sel_kv, state_win_kv, page_table, c_prompt, c_sample, rel_bias_table, a_norm_g, a_w_ada, a_b_ada, a_w_in, a_sinks, a_w_out, kv_norm_g, kv_w_ada, kv_b_ada, kv_w_in, cmp_pos, cmp_w1, cmp_w2, b_norm_g, b_w_ada, b_b_ada, b_w_in, b_w_out, final_norm_g):
    n_p, seq, d = x_prompt.shape
    n_s = x_sample.shape[0]
    n_pages = page_table.shape[1]
    past = n_pages * PAGE_SIZE
    n_a, n_b = a_w_in.shape[0], b_w_in.shape[0]
    tq = Q_BLOCK
    assert x_sample.shape[1] == 1 and seq % 512 == 0 and n_s % 8 == 0
    assert seq // SEL_BLOCK >= SEL_TOPK and past // SEL_BLOCK >= SEL_TOPK
    assert state_a_kv.shape[2] == WINDOW_A and state_win_kv.shape[1] == WINDOW_B

    c_all = jnp.concatenate([c_prompt, jnp.zeros((8 - n_p, d), F32), c_sample], axis=0)
    ada_a = _ada(c_all, a_w_ada, a_b_ada)
    ada_b = _ada(c_all, b_w_ada, b_b_ada)
    ada_kv = _ada(c_all, kv_w_ada[None], kv_b_ada[None])

    def mods(ada, parts):
        pr = [ada[:n_p, None, i * d:(i + 1) * d] for i in range(parts)]
        sm = [ada[None, 8:8 + n_s, i * d:(i + 1) * d] for i in range(parts)]
        return pr, sm

    table = rel_bias_table
    xs = x_sample.reshape(1, n_s, d)
    xp = x_prompt

    bias_a = _band_bias(table, BAND_TQ, WINDOW_A)
    dbias_a, dnew = _decode_bias(table, WINDOW_A, WINDOW_A)
    a_new_p, a_new_s = [], []
    for i in range(n_a):
        w_in = a_w_in[i]
        ws = [_pair_cols(w_in[:, :HD]).astype(BF16), w_in[:, HD:HD + KV_ROW].astype(BF16),
              _pair_cols(w_in[:, HD + KV_ROW:]).astype(BF16)]
        outs = [(0, 0, HD, BF16), (1, 0, KV_ROW, F32), (1, 0, KVD, BF16), (1, KVD, KV_ROW, BF16), (2, 0, HD, BF16)]
        w_out = _pair_rows(a_w_out[i]).astype(BF16)
        (sh_p, sc_p, gt_p), (sh_s, sc_s, gt_s) = mods(ada_a[i], 3)
        q, kv, kb, vb, z = _mod_proj(xp, a_norm_g[i], sh_p, sc_p, ws, outs, 512)
        vb0, vb1 = _with_ones(vb)
        o = _band_attn(q, _front_pad(kb, WINDOW_A), _front_pad(vb0, WINDOW_A), _front_pad(vb1, WINDOW_A), bias_a,
                       _stack_rows(a_sinks[i], BAND_TQ), WINDOW_A)
        xp = _finish(xp, gt_p, z, [o], w_out)
        a_new_p.append(kv[:, seq - WINDOW_A:].reshape(n_p, WINDOW_A, 2, N_KV_HEADS, HEAD_DIM))
        q, kv, _, _, z = _mod_proj(xs, a_norm_g[i], sh_s, sc_s, ws, outs, 128)
        new = kv.reshape(n_s, 1, KV_ROW)
        o = _dec_attn(q.reshape(n_s, GROUP, LANES), _dim_token(state_a_kv[i]), new, dbias_a, dnew,
                      a_sinks[i].astype(F32).reshape(N_KV_HEADS, GROUP, 1))
        xs = _finish(xs, gt_s, z, [o.reshape(1, n_s, HD)], w_out)
        a_new_s.append(jnp.concatenate([state_a_kv[i][:, 1:], kv.reshape(n_s, 1, 2, N_KV_HEADS, HEAD_DIM)], axis=1))

    (sh_p, sc_p), (sh_s, sc_s) = mods(ada_kv[0], 2)
    w_kv = [kv_w_in.astype(BF16)]
    outs = [(0, KV_ROW * b, KV_ROW * (b + 1), F32) for b in range(N_BRANCH)]
    outs += [(0, KV_ROW + KVD * h, KV_ROW + KVD * (h + 1), BF16) for h in range(4)]
    cmp_p, sel_p, win_p, sel_kp, sel_vp, win_kp, win_vp = _mod_proj(xp, kv_norm_g, sh_p, sc_p, w_kv, outs, 512)
    cmp_s, sel_s, win_s = _mod_proj(xs, kv_norm_g, sh_s, sc_s, w_kv, outs[:N_BRANCH], 128)
    cmp_s, sel_s, win_s = (r.reshape(n_s, 1, KV_ROW) for r in (cmp_s, sel_s, win_s))

    eye_g = jnp.eye(N_KV_HEADS, dtype=F32)
    w1e = jnp.einsum('kbdh,gG->bkgdGh', cmp_w1, eye_g).reshape(
        HALF_ROWS, KVD, N_KV_HEADS * CMP_HIDDEN).astype(BF16)
    w2e = jnp.einsum('khd,gG->kghGd', cmp_w2, eye_g).reshape(2, N_KV_HEADS * CMP_HIDDEN, KVD).astype(BF16)
    pos_e = jnp.broadcast_to(cmp_pos[:, :, None, :], (CMP_BLOCK, 2, N_KV_HEADS, HEAD_DIM)).reshape(HALF_ROWS, KVD)
    comp_p = _compress(cmp_p, pos_e, w1e, w2e)
    comp_past = _compress_paged(_dim_token(cache_cmp_kv), page_table, pos_e, w1e, w2e)
    tail = jnp.pad(cmp_s, ((0, 0), (0, SEL_BLOCK - 1), (0, 0))).reshape(1, n_s * SEL_BLOCK, KV_ROW)
    comp_tail = _compress(tail, pos_e, w1e, w2e).reshape(n_s, SEL_BLOCK // CMP_BLOCK, KV_ROW)

    nsp_p = -(-(seq // SEL_BLOCK) // LANES) * LANES
    nsp_s = -(-(past // SEL_BLOCK + 1) // LANES) * LANES
    ck_p, cv_p = _even_odd(comp_p, nsp_p)
    n_sel_past = past // SEL_BLOCK
    fill = jnp.zeros((n_s, nsp_s - n_sel_past - 1, KV_ROW), F32)
    comp_s = jnp.concatenate([comp_past[:, :n_sel_past], comp_tail[:, 0:1], fill,
                              comp_past[:, n_sel_past:], comp_tail[:, 1:2], fill], axis=1).astype(BF16)
    ck_s, cv_s = comp_s[:, :, :KVD], comp_s[:, :, KVD:]

    sel_vp0, sel_vp1 = (_front_pad(v, tq) for v in _with_ones(sel_vp))
    win_vp0, win_vp1 = (_front_pad(v, WINDOW_B) for v in _with_ones(win_vp))
    sel_kp, win_kp = _front_pad(sel_kp, tq), _front_pad(win_kp, WINDOW_B)
    key_block = (jnp.arange(tq + seq) - tq) // SEL_BLOCK
    is_block = (key_block[:, None] == jnp.arange(nsp_p)[None, :]) & (jnp.arange(tq + seq) >= tq)[:, None]
    et = jnp.where(is_block, NEG, 0.0).astype(BF16)
    nb_sel = _near_bias(table, tq)
    bias_b = _band_bias(table, tq, WINDOW_B)
    dbias_b, _ = _decode_bias(table, WINDOW_B, WINDOW_B)
    win_buf_t = _dim_token(state_win_kv)
    kpos = np.arange(past).reshape(n_pages, PAGE_SIZE)
    bt = jnp.transpose(_bias_of(table, past - kpos), (0, 2, 1))
    sel_pool_t = _dim_token(cache_sel_kv)

    head_of = _PERM // HEAD_DIM
    e_np = np.zeros((LANES, N_BRANCH * HD), np.float32)
    for br in range(N_BRANCH):
        e_np[br * N_HEADS + head_of, br * HD + np.arange(HD)] = 1.0
    e_gl = jnp.asarray(e_np, dtype=BF16)

    y_p = y_s = None
    for j in range(n_b):
        w_in = b_w_in[j]
        w_z = _pair_cols(w_in[:, HD:HD + N_BRANCH * HD].reshape(d, N_BRANCH, HD)).reshape(d, N_BRANCH * HD)
        ws = [_pair_cols(w_in[:, :HD]).astype(BF16), w_z.astype(BF16),
              jnp.pad(w_in[:, HD + N_BRANCH * HD:], ((0, 0), (0, LANES - N_BRANCH * N_HEADS))).astype(BF16)]
        outs = [(0, 0, HD, BF16), (1, 0, N_BRANCH * HD, BF16), (2, 0, LANES, F32)]
        w_out = _pair_rows(b_w_out[j]).astype(BF16)
        fg = final_norm_g if j == n_b - 1 else None
        (sh_p, sc_p, gt_p), (sh_s, sc_s, gt_s) = mods(ada_b[j], 3)
        q, z, gl = _mod_proj(xp, b_norm_g[j], sh_p, sc_p, ws, outs, 256)
        o_cmp, sel = _cmp_topk(q, ck_p, cv_p)
        o_sel = _sel_attn(q, sel, sel_kp, sel_vp0, sel_vp1, et, nb_sel)
        o_win = _band_attn(q, win_kp, win_vp0, win_vp1, bias_b, None, WINDOW_B)
        r = _finish(xp, gt_p, z, [o_cmp, o_sel, o_win], w_out, gl, e_gl, fg, 256)
        xp, y_p = r if fg is not None else (r, None)
        q, z, gl = _mod_proj(xs, b_norm_g[j], sh_s, sc_s, ws, outs, 128)
        q = q.reshape(n_s, GROUP, LANES)
        o_cmp, idx = _dec_cmp_topk(q, ck_s, cv_s, past)
        idx = idx.reshape(n_s, N_KV_HEADS, LANES)[:, :, :SEL_TOPK].reshape(-1)
        o_sel = _dec_sel(idx, page_table, q, sel_s, bt, dnew, sel_pool_t)
        o_win = _dec_attn(q, win_buf_t, win_s, dbias_b, dnew, None)
        r = _finish(xs, gt_s, z, [o.reshape(1, n_s, HD) for o in (o_cmp, o_sel, o_win)], w_out, gl, e_gl, fg, 128)
        xs, y_s = r if fg is not None else (r, None)

    kv5 = lambda r: r.reshape(r.shape[0], r.shape[1], 2, N_KV_HEADS, HEAD_DIM)
    new_win_s = jnp.concatenate([state_win_kv[:, 1:], kv5(win_s)], axis=1)
    return (y_p, y_s.reshape(n_s, 1, d), jnp.stack(a_new_p), jnp.stack(a_new_s), kv5(cmp_p), kv5(cmp_s),
            kv5(sel_p), kv5(sel_s), kv5(win_p[:, seq - WINDOW_B:]), new_win_s)
```

```python
import functools
import math

import numpy as np
import jax
import jax.numpy as jnp
from jax import lax
from jax.experimental import pallas as pl
from jax.experimental.pallas import tpu as pltpu

F32, BF16 = jnp.float32, jnp.bfloat16

D_MODEL = 1024
N_HEADS = 16
HEAD_DIM = 64
N_KV_HEADS = 2
GROUP = N_HEADS // N_KV_HEADS
HD = N_HEADS * HEAD_DIM
KVD = N_KV_HEADS * HEAD_DIM
KV_ROW = 2 * KVD
WINDOW_A = 128
WINDOW_B = 512
Q_BLOCK = 128
CMP_BLOCK = 32
SEL_BLOCK = 64
SEL_TOPK = 16
CMP_HIDDEN = 2 * HEAD_DIM
N_BRANCH = 3
N_BUCKETS = 32
T5_MAX_DISTANCE = 128
RMS_EPS = 1e-6
ATTN_SCALE = HEAD_DIM ** -0.5
PAGE_SIZE = 128
LANES = 128
NEG = -1e30
VMEM_LIMIT = 56 * 1024 * 1024

_PERM = np.array([(GROUP * g + c) * HEAD_DIM + d for c in range(GROUP) for g in range(N_KV_HEADS)
                  for d in range(HEAD_DIM)], dtype=np.int32)


def _params(sem):
    return pltpu.CompilerParams(dimension_semantics=sem, vmem_limit_bytes=VMEM_LIMIT)


def _silu(x):
    return x * jax.nn.sigmoid(x)


def _ada_kernel(c_ref, w_ref, b_ref, o_ref):
    a = _silu(c_ref[...]).astype(BF16)
    o_ref[0] = jnp.dot(a, w_ref[0].astype(BF16), preferred_element_type=F32) + b_ref[0]


def _ada(c_all, w, b):
    nl, d, n = w.shape
    m = c_all.shape[0]
    tn = 512
    return pl.pallas_call(
        _ada_kernel,
        grid=(nl, n // tn),
        in_specs=[pl.BlockSpec((m, d), lambda l, j: (0, 0)),
                  pl.BlockSpec((1, d, tn), lambda l, j: (l, 0, j)),
                  pl.BlockSpec((1, 1, tn), lambda l, j: (l, 0, j))],
        out_specs=pl.BlockSpec((1, m, tn), lambda l, j: (l, 0, j)),
        out_shape=jax.ShapeDtypeStruct((nl, m, n), F32),
        compiler_params=_params(("parallel", "parallel")),
        name="ada_params",
    )(c_all, w, b.reshape(nl, 1, n))


def _mod_proj_kernel(x_ref, g_ref, sh_ref, sc_ref, *refs, nw, outs):
    w_refs, o_refs = refs[:nw], refs[nw:]
    x = x_ref[0]
    y = x * lax.rsqrt(jnp.mean(x * x, axis=-1, keepdims=True) + RMS_EPS) * g_ref[...]
    h = (y * (1.0 + sc_ref[0]) + sh_ref[0]).astype(BF16)
    res = {}
    for (wi, c0, c1, dt), o_ref in zip(outs, o_refs):
        if wi not in res:
            res[wi] = jnp.dot(h, w_refs[wi][...], preferred_element_type=F32)
        o_ref[0] = res[wi][:, c0:c1].astype(dt)


def _mod_proj(x, g, shift, scale, ws, outs, tm):
    nb, t, d = x.shape
    tm = min(tm, t)
    if shift.shape[1] == 1:
        mod_spec = pl.BlockSpec((1, 1, d), lambda n, i: (n, 0, 0))
    else:
        mod_spec = pl.BlockSpec((1, tm, d), lambda n, i: (n, i, 0))
    in_specs = [pl.BlockSpec((1, tm, d), lambda n, i: (n, i, 0)),
                pl.BlockSpec((1, d), lambda n, i: (0, 0)), mod_spec, mod_spec]
    in_specs += [pl.BlockSpec(w.shape, lambda n, i: (0, 0)) for w in ws]
    out_specs = [pl.BlockSpec((1, tm, c1 - c0), lambda n, i: (n, i, 0)) for (_, c0, c1, _) in outs]
    out_shape = [jax.ShapeDtypeStruct((nb, t, c1 - c0), dt) for (_, c0, c1, dt) in outs]
    return pl.pallas_call(
        functools.partial(_mod_proj_kernel, nw=len(ws), outs=tuple(outs)),
        grid=(nb, t // tm),
        in_specs=in_specs, out_specs=out_specs, out_shape=out_shape,
        compiler_params=_params(("parallel", "parallel")),
        name="mod_proj",
    )(x, g.reshape(1, d), shift, scale, *ws)


def _stack_queries(q_ref, qs_ref, tq):
    lane = lax.broadcasted_iota(jnp.int32, (tq, LANES), 1)
    for c in range(GROUP):
        qc = (q_ref[0, :, LANES * c:LANES * (c + 1)].astype(F32) * ATTN_SCALE)
        qs_ref[0, c * tq:(c + 1) * tq, :LANES] = jnp.where(lane < HEAD_DIM, qc, 0.0).astype(BF16)
        qs_ref[1, c * tq:(c + 1) * tq, :LANES] = jnp.where(lane >= HEAD_DIM, qc, 0.0).astype(BF16)


def _store_paired(o_ref, o0, o1, tq):
    lane = lax.broadcasted_iota(jnp.int32, o0.shape, 1)
    oc = jnp.where(lane < HEAD_DIM, o0, o1)
    for c in range(GROUP):
        o_ref[0, :, LANES * c:LANES * (c + 1)] = oc[c * tq:(c + 1) * tq].astype(o_ref.dtype)


def _nt_dot(a, b):
    return lax.dot_general(a, b, (((1,), (1,)), ((), ())), preferred_element_type=F32)


def _row_max(s):
    cm = s[:, :LANES]
    for c in range(1, s.shape[1] // LANES):
        cm = jnp.maximum(cm, s[:, LANES * c:LANES * (c + 1)])
    return jnp.max(cm, axis=-1, keepdims=True)


def _lanes(m, width):
    return m if width == LANES else jnp.concatenate([m] * (width // LANES), axis=1)


def _normalize(acc):
    return acc / pltpu.roll(acc, HEAD_DIM, axis=1)


def _with_ones(v):
    lane = lax.broadcasted_iota(jnp.int32, v.shape, v.ndim - 1)
    one = jnp.ones_like(v)
    return jnp.where(lane < HEAD_DIM, v, one), jnp.where(lane >= HEAD_DIM, v, one)


BAND_TQ = 256

def _band_attn_kernel(*refs, tq, window, use_sink):
    if use_sink:
        q_ref, k_ref, v0_ref, v1_ref, bias_ref, sink_ref, o_ref, qs_ref = refs
    else:
        q_ref, k_ref, v0_ref, v1_ref, bias_ref, o_ref, qs_ref = refs
    s0 = pl.multiple_of(pl.program_id(1) * tq, tq)
    band = window + tq
    rows = GROUP * tq
    _stack_queries(q_ref, qs_ref, tq)
    kpos = s0 - window + lax.broadcasted_iota(jnp.int32, (1, band), 1)
    front = jnp.where(kpos >= 0, 0.0, NEG)
    lane = lax.broadcasted_iota(jnp.int32, (rows, LANES), 1)
    split = (band // 2 + LANES - 1) // LANES * LANES
    pieces = [(0, band)] if band <= 4 * LANES else [(0, split), (split, band - split)]
    outs = []
    for g, v_ref in enumerate((v0_ref, v1_ref)):
        m = acc = None
        for c0, w in pieces:
            kb = k_ref[0, pl.ds(s0 + c0, w), :]
            s = _nt_dot(qs_ref[g], kb) + bias_ref[g, :, c0:c0 + w] + front[:, c0:c0 + w]
            m_new = jnp.broadcast_to(_row_max(s), (rows, LANES))
            if m is not None:
                m_new = jnp.maximum(m, m_new)
            elif use_sink:
                m_new = jnp.maximum(m_new, sink_ref[g])
            pv = jnp.dot(jnp.exp(s - _lanes(m_new, w)).astype(BF16), v_ref[0, pl.ds(s0 + c0, w), :],
                         preferred_element_type=F32)
            acc = pv if m is None else jnp.exp(m - m_new) * acc + pv
            m = m_new
        if use_sink:
            acc = acc + jnp.where((lane < HEAD_DIM) == (g == 0), 0.0, jnp.exp(sink_ref[g] - m))
        outs.append(_normalize(acc))
    _store_paired(o_ref, outs[0], outs[1], tq)


def _band_attn(q, kp, vp0, vp1, bias, sink, window):
    n, s, _ = q.shape
    tq = bias.shape[1] // GROUP
    band = window + tq
    use_sink = sink is not None
    once = pl.Buffered(1)
    kv_spec = pl.BlockSpec((1, window + s, KVD), lambda b, i: (b, 0, 0))
    in_specs = [pl.BlockSpec((1, tq, HD), lambda b, i: (b, i, 0)), kv_spec, kv_spec, kv_spec,
                pl.BlockSpec((N_KV_HEADS, GROUP * tq, band), lambda b, i: (0, 0, 0), pipeline_mode=once)]
    args = [q, kp, vp0, vp1, bias]
    if use_sink:
        in_specs.append(pl.BlockSpec((N_KV_HEADS, GROUP * tq, LANES), lambda b, i: (0, 0, 0), pipeline_mode=once))
        args.append(sink)
    return pl.pallas_call(
        functools.partial(_band_attn_kernel, tq=tq, window=window, use_sink=use_sink),
        grid=(n, s // tq),
        in_specs=in_specs,
        out_specs=pl.BlockSpec((1, tq, HD), lambda b, i: (b, i, 0)),
        out_shape=jax.ShapeDtypeStruct((n, s, HD), BF16),
        scratch_shapes=[pltpu.VMEM((N_KV_HEADS, GROUP * tq, LANES), BF16)],
        compiler_params=_params(("parallel", "arbitrary")),
        name="band_attn",
    )(*args)


def _cmp_softmax(s3, negm, has_valid):
    def fold(x, op):
        acc = x[..., :LANES]
        for c in range(1, x.shape[-1] // LANES):
            acc = op(acc, x[..., LANES * c:LANES * (c + 1)])
        return acc

    s3 = s3 + negm
    m = jnp.max(fold(s3, jnp.maximum), axis=-1, keepdims=True)
    m = jnp.where(has_valid, m, 0.0)
    e = jnp.exp(s3 - m)
    den = jnp.sum(fold(e, jnp.add), axis=-1, keepdims=True)
    return e / jnp.where(den > 0, den, 1.0)


def _cmp_mask(t, nsp):
    u = lax.broadcasted_iota(jnp.int32, (t.shape[0], 2 * nsp), 1)
    cend = jnp.where(u < nsp, SEL_BLOCK * u + (CMP_BLOCK - 1), SEL_BLOCK * (u - nsp) + (SEL_BLOCK - 1))
    return jnp.where(cend <= t, 0.0, NEG), t >= CMP_BLOCK - 1


def _select_blocks(imp, t):
    rows, nsp = imp.shape
    j = lax.broadcasted_iota(jnp.int32, (rows, nsp), 1)
    jf = j.astype(F32)
    cur = t // SEL_BLOCK
    forced = (j == 0) | (j == cur) | (j == cur - 1)
    work = jnp.where(forced, jnp.inf, jnp.where(j <= cur, imp, -jnp.inf))
    sel = jnp.zeros((rows, nsp), F32)
    lane = lax.broadcasted_iota(jnp.int32, (rows, LANES), 1)
    idx = jnp.full((rows, LANES), -1.0, F32)
    for k in range(SEL_TOPK):
        m = jnp.max(work, axis=-1, keepdims=True)
        first = jnp.min(jnp.where(work == m, jf, float(nsp)), axis=-1, keepdims=True)
        hit = jf == first
        ok = m > -jnp.inf
        sel = jnp.where(hit & ok, 1.0, sel)
        idx = jnp.where(lane == k, jnp.where(ok, first, -1.0), idx)
        work = jnp.where(hit, -jnp.inf, work)
    return sel, idx


def _select_blocks_t(imp_t, t):
    nsp, cols = imp_t.shape
    j = lax.broadcasted_iota(jnp.int32, (nsp, cols), 0)
    jf = j.astype(F32)
    cur = t // SEL_BLOCK
    forced = (j == 0) | (j == cur) | (j == cur - 1)
    work = jnp.where(forced, jnp.inf, jnp.where(j <= cur, imp_t, -jnp.inf))
    sel = jnp.zeros((nsp, cols), F32)
    for _ in range(SEL_TOPK):
        m = jnp.max(work, axis=0, keepdims=True)
        first = jnp.min(jnp.where(work == m, jf, float(nsp)), axis=0, keepdims=True)
        hit = jf == first
        sel = jnp.where(hit & (m > -jnp.inf), 1.0, sel)
        work = jnp.where(hit, -jnp.inf, work)
    return sel


def _cmp_topk_kernel(q_ref, ck_ref, cv_ref, o_ref, sel_ref, qs_ref, *, tq, nsp):
    s0 = pl.program_id(1) * tq
    _stack_queries(q_ref, qs_ref, tq)
    t = s0 + lax.broadcasted_iota(jnp.int32, (tq, 1), 0)
    t_row = s0 + lax.broadcasted_iota(jnp.int32, (1, tq), 1)
    negm, has_valid = _cmp_mask(t, nsp)
    ck, cv = ck_ref[0], cv_ref[0]
    outs = []
    for g in range(N_KV_HEADS):
        s3 = _nt_dot(qs_ref[g], ck).reshape(GROUP, tq, 2 * nsp)
        p = _cmp_softmax(s3, negm[None], has_valid[None])
        outs.append(jnp.dot(p.reshape(GROUP * tq, 2 * nsp).astype(BF16), cv, preferred_element_type=F32))
        ph = jnp.sum(p, axis=0)
        sel_t = _select_blocks_t((ph[:, :nsp] + ph[:, nsp:]).T, t_row)
        sel_ref[0, g] = sel_t.T.astype(sel_ref.dtype)
    _store_paired(o_ref, outs[0], outs[1], tq)


def _cmp_topk(q, ck, cv):
    n, s, _ = q.shape
    tq = Q_BLOCK
    nsp = ck.shape[1] // 2
    return pl.pallas_call(
        functools.partial(_cmp_topk_kernel, tq=tq, nsp=nsp),
        grid=(n, s // tq),
        in_specs=[pl.BlockSpec((1, tq, HD), lambda b, i: (b, i, 0)),
                  pl.BlockSpec((1, 2 * nsp, KVD), lambda b, i: (b, 0, 0)),
                  pl.BlockSpec((1, 2 * nsp, KVD), lambda b, i: (b, 0, 0))],
        out_specs=[pl.BlockSpec((1, tq, HD), lambda b, i: (b, i, 0)),
                   pl.BlockSpec((1, N_KV_HEADS, tq, nsp), lambda b, i: (b, 0, i, 0))],
        out_shape=[jax.ShapeDtypeStruct((n, s, HD), BF16),
                   jax.ShapeDtypeStruct((n, N_KV_HEADS, s, nsp), BF16)],
        scratch_shapes=[pltpu.VMEM((N_KV_HEADS, GROUP * tq, LANES), BF16)],
        compiler_params=_params(("parallel", "arbitrary")),
        name="cmp_topk",
    )(q, ck, cv)


FAR_TILES = 8


def _sel_attn_kernel(q_ref, sel_ref, k_ref, v0_ref, v1_ref, et_ref, nb_ref, o_ref, qs_ref, m_ref, acc_ref, *, tq):
    qt = pl.program_id(1)
    s0 = pl.multiple_of(qt * tq, tq)
    rows = GROUP * tq
    v_refs = (v0_ref, v1_ref)
    _stack_queries(q_ref, qs_ref, tq)
    for g in range(N_KV_HEADS):
        unselected = 1.0 - sel_ref[0, g]
        for c in range(GROUP):
            qs_ref[g, c * tq:(c + 1) * tq, LANES:] = unselected
    kpos = s0 - tq + lax.broadcasted_iota(jnp.int32, (1, 2 * tq), 1)
    front = jnp.where(kpos >= 0, 0.0, NEG)

    def key_tile(g, off, width, bias, first):
        kf = jnp.concatenate([k_ref[0, pl.ds(off, width), :], et_ref[pl.ds(off, width), :]], axis=1)
        vf = v_refs[g][0, pl.ds(off, width), :]
        s = _nt_dot(qs_ref[g], kf)
        if bias is not None:
            lead = width - 2 * tq
            near = s[:, lead:].reshape(GROUP, tq, 2 * tq) + bias.reshape(GROUP, tq, 2 * tq)
            near = near.reshape(rows, 2 * tq) + front
            s = near if lead == 0 else jnp.concatenate([s[:, :lead], near], axis=1)
        rm = _row_max(s)
        if first:
            m_new = jnp.broadcast_to(rm, (rows, LANES))
        else:
            m_old = m_ref[g]
            m_new = jnp.maximum(m_old, rm)
        pv = jnp.dot(jnp.exp(s - _lanes(m_new, width)).astype(BF16), vf, preferred_element_type=F32)
        acc_ref[g] = pv if first else jnp.exp(m_old - m_new) * acc_ref[g] + pv
        m_ref[g] = m_new

    n_far = jnp.maximum(qt - 1, 0)
    n_wide = n_far // FAR_TILES
    n_rest = n_far - n_wide * FAR_TILES

    for rest in range(FAR_TILES):
        @pl.when(n_rest == rest)
        def _():
            off = pl.multiple_of(s0 - rest * tq, tq)
            for g in range(N_KV_HEADS):
                key_tile(g, off, (rest + 2) * tq, nb_ref[g], True)

    def wide(j, carry):
        off = pl.multiple_of(tq + j * (FAR_TILES * tq), tq)
        for g in range(N_KV_HEADS):
            key_tile(g, off, FAR_TILES * tq, None, False)
        return carry

    lax.fori_loop(0, n_wide, wide, 0)
    _store_paired(o_ref, _normalize(acc_ref[0]), _normalize(acc_ref[1]), tq)


def _sel_attn(q, sel, kp, vp0, vp1, et, nb):
    n, s, _ = q.shape
    tq = Q_BLOCK
    nsp = sel.shape[-1]
    assert nsp == LANES
    kv_spec = pl.BlockSpec((1, tq + s, KVD), lambda b, i: (b, 0, 0))
    return pl.pallas_call(
        functools.partial(_sel_attn_kernel, tq=tq),
        grid=(n, s // tq),
        in_specs=[pl.BlockSpec((1, tq, HD), lambda b, i: (b, i, 0)),
                  pl.BlockSpec((1, N_KV_HEADS, tq, nsp), lambda b, i: (b, 0, i, 0)),
                  kv_spec, kv_spec, kv_spec,
                  pl.BlockSpec((tq + s, nsp), lambda b, i: (0, 0)),
                  pl.BlockSpec((N_KV_HEADS, GROUP * tq, 2 * tq), lambda b, i: (0, 0, 0))],
        out_specs=pl.BlockSpec((1, tq, HD), lambda b, i: (b, i, 0)),
        out_shape=jax.ShapeDtypeStruct((n, s, HD), BF16),
        scratch_shapes=[pltpu.VMEM((N_KV_HEADS, GROUP * tq, 2 * LANES), BF16),
                        pltpu.VMEM((N_KV_HEADS, GROUP * tq, LANES), F32),
                        pltpu.VMEM((N_KV_HEADS, GROUP * tq, LANES), F32)],
        compiler_params=_params(("parallel", "arbitrary")),
        name="sel_attn",
    )(q, sel, kp, vp0, vp1, et, nb)


def _finish_kernel(*refs, n_branch, final):
    x_ref, gate_ref, z_ref = refs[:3]
    pos = 3
    if n_branch > 1:
        gl_ref, e_ref = refs[pos:pos + 2]
        pos += 2
    o_refs = refs[pos:pos + n_branch]
    pos += n_branch
    w_ref = refs[pos]
    pos += 1
    if final:
        fg_ref = refs[pos]
        pos += 1
    out_ref = refs[pos]
    mix = None
    if n_branch > 1:
        sg = jax.nn.sigmoid(gl_ref[0]).astype(BF16)
    for br in range(n_branch):
        t = o_refs[br][0].astype(F32) * _silu(z_ref[0, :, HD * br:HD * (br + 1)].astype(F32))
        if n_branch > 1:
            t = t * jnp.dot(sg, e_ref[:, HD * br:HD * (br + 1)], preferred_element_type=F32)
        mix = t if mix is None else mix + t
    upd = jnp.dot(mix.astype(BF16), w_ref[...], preferred_element_type=F32)
    xn = x_ref[0] + gate_ref[0] * upd
    out_ref[0] = xn
    if final:
        y = xn * lax.rsqrt(jnp.mean(xn * xn, axis=-1, keepdims=True) + RMS_EPS) * fg_ref[...]
        refs[pos + 1][0] = y


def _finish(x, gate, z, os_, w, gl=None, e=None, final_g=None, tm=512):
    nb, t, d = x.shape
    tm = min(tm, t)
    n_branch = len(os_)
    final = final_g is not None
    row = lambda c: pl.BlockSpec((1, tm, c), lambda n, i: (n, i, 0))
    if gate.shape[1] == 1:
        gate_spec = pl.BlockSpec((1, 1, d), lambda n, i: (n, 0, 0))
    else:
        gate_spec = row(d)
    in_specs = [row(d), gate_spec, row(z.shape[-1])]
    args = [x, gate, z]
    if n_branch > 1:
        in_specs += [row(gl.shape[-1]), pl.BlockSpec(e.shape, lambda n, i: (0, 0))]
        args += [gl, e]
    in_specs += [row(HD)] * n_branch + [pl.BlockSpec(w.shape, lambda n, i: (0, 0))]
    args += list(os_) + [w]
    out_specs, out_shape = [row(d)], [jax.ShapeDtypeStruct((nb, t, d), F32)]
    if final:
        in_specs.append(pl.BlockSpec((1, d), lambda n, i: (0, 0)))
        args.append(final_g.reshape(1, d))
        out_specs.append(row(d))
        out_shape.append(jax.ShapeDtypeStruct((nb, t, d), F32))
    res = pl.pallas_call(
        functools.partial(_finish_kernel, n_branch=n_branch, final=final),
        grid=(nb, t // tm),
        in_specs=in_specs, out_specs=out_specs, out_shape=out_shape,
        compiler_params=_params(("parallel", "parallel")),
        name="finish",
    )(*args)
    return res if final else res[0]


HALF_ROWS = 2 * CMP_BLOCK


def _compress_rows(xb, pos_ref, w1_ref, w2_ref, side=None):
    outs = []
    for k in range(2):
        hid = None
        for b in range(CMP_BLOCK):
            r = 2 * b + k
            xv = (xb(r) + pos_ref[r:r + 1, :]).astype(BF16)
            part = jnp.dot(xv, w1_ref[r], preferred_element_type=F32)
            hid = part if hid is None else hid + part
            if side is not None:
                side(k * CMP_BLOCK + b)
        outs.append(jnp.dot(_silu(hid).astype(BF16), w2_ref[k], preferred_element_type=F32))
    return jnp.concatenate(outs, axis=1)


def _compress_kernel(x_ref, pos_ref, w1_ref, w2_ref, o_ref, *, nblk):
    o_ref[0] = _compress_rows(lambda r: x_ref[0, pl.ds(r, nblk, stride=HALF_ROWS), :], pos_ref, w1_ref, w2_ref)


def _compress(rows, pos_e, w1e, w2e):
    nb, t, _ = rows.shape
    nblk = t // CMP_BLOCK
    return pl.pallas_call(
        functools.partial(_compress_kernel, nblk=nblk),
        grid=(nb,),
        in_specs=[pl.BlockSpec((1, 2 * t, KVD), lambda n: (n, 0, 0)),
                  pl.BlockSpec(pos_e.shape, lambda n: (0, 0)),
                  pl.BlockSpec(w1e.shape, lambda n: (0, 0, 0)),
                  pl.BlockSpec(w2e.shape, lambda n: (0, 0, 0))],
        out_specs=pl.BlockSpec((1, nblk, KV_ROW), lambda n: (n, 0, 0)),
        out_shape=jax.ShapeDtypeStruct((nb, nblk, KV_ROW), F32),
        compiler_params=_params(("parallel",)),
        name="compress",
    )(rows.reshape(nb, 2 * t, KVD), pos_e, w1e, w2e)


def _page_copies(pt_ref, pool_ref, raw_ref, sem_ref, n, slot, n_pages):
    return [pltpu.make_async_copy(pool_ref.at[pt_ref[n * n_pages + p]], raw_ref.at[slot, p], sem_ref.at[slot, p])
            for p in range(n_pages)]


CMP_PITCH = CMP_BLOCK + 2
BLOCKS_PER_CMP_PAGE = PAGE_SIZE // CMP_BLOCK


def _compress_paged_kernel(pt_ref, pool_ref, pos_ref, w1_ref, w2_ref, o_ref, raw_ref, x_ref, sem_ref, *,
                           n_pages, n_samples):
    n = pl.program_id(0)
    cur = n % 2
    nxt = 1 - cur
    copies = lambda sample, slot: _page_copies(pt_ref, pool_ref, raw_ref, sem_ref, sample, slot, n_pages)

    def to_token_rows(slot, p):
        for k in range(2):
            t = raw_ref[slot, p, KVD * k:KVD * (k + 1), :].T
            for c in range(BLOCKS_PER_CMP_PAGE):
                row0 = (p * BLOCKS_PER_CMP_PAGE + c) * CMP_PITCH
                x_ref[slot, k, pl.ds(row0, CMP_BLOCK), :] = t[c * CMP_BLOCK:(c + 1) * CMP_BLOCK]

    @pl.when(n == 0)
    def _():
        for cp in copies(0, 0):
            cp.start()
        if n_samples > 1:
            for cp in copies(1, 1):
                cp.start()
        for cp in copies(0, 0):
            cp.wait()

        def first(p, carry):
            to_token_rows(0, p)
            return carry

        lax.fori_loop(0, n_pages, first, 0)

    @pl.when(n + 1 < n_samples)
    def _():
        for cp in copies(n + 1, nxt):
            cp.wait()

    @pl.when(n + 2 < n_samples)
    def _():
        for cp in copies(n + 2, cur):
            cp.start()

    per_call = -(-n_pages // HALF_ROWS)

    def side(i):
        for p in range(i * per_call, min((i + 1) * per_call, n_pages)):
            to_token_rows(nxt, p)

    half = n_pages * BLOCKS_PER_CMP_PAGE // 2

    def row_of_blocks(r):
        k, b = r % 2, r // 2
        return jnp.concatenate([x_ref[cur, k, pl.ds(b, half, stride=2 * CMP_PITCH), :],
                                x_ref[cur, k, pl.ds(CMP_PITCH + b, half, stride=2 * CMP_PITCH), :]], axis=0)

    o_ref[0] = _compress_rows(row_of_blocks, pos_ref, w1_ref, w2_ref, side if n_samples > 1 else None)


def _compress_paged(pool_t, page_table, pos_e, w1e, w2e):
    ns, n_pages = page_table.shape
    past = n_pages * PAGE_SIZE
    nblk = past // CMP_BLOCK
    grid_spec = pltpu.PrefetchScalarGridSpec(
        num_scalar_prefetch=1,
        grid=(ns,),
        in_specs=[pl.BlockSpec(memory_space=pl.ANY),
                  pl.BlockSpec(pos_e.shape, lambda n, pt: (0, 0)),
                  pl.BlockSpec(w1e.shape, lambda n, pt: (0, 0, 0)),
                  pl.BlockSpec(w2e.shape, lambda n, pt: (0, 0, 0))],
        out_specs=pl.BlockSpec((1, nblk, KV_ROW), lambda n, pt: (n, 0, 0)),
        scratch_shapes=[pltpu.VMEM((2, n_pages, KV_ROW, PAGE_SIZE), F32),
                        pltpu.VMEM((2, 2, nblk * CMP_PITCH, KVD), F32),
                        pltpu.SemaphoreType.DMA((2, n_pages))])
    return pl.pallas_call(
        functools.partial(_compress_paged_kernel, n_pages=n_pages, n_samples=ns),
        grid_spec=grid_spec,
        out_shape=jax.ShapeDtypeStruct((ns, nblk, KV_ROW), F32),
        compiler_params=_params(("arbitrary",)),
        name="compress_paged",
    )(page_table.reshape(-1), pool_t, pos_e, w1e, w2e)


def _decode_queries(q):
    lane = lax.broadcasted_iota(jnp.int32, q.shape, 1)
    qs = q * ATTN_SCALE
    return [jnp.where(lane < HEAD_DIM, qs, 0.0), jnp.where(lane >= HEAD_DIM, qs, 0.0)]


def _decode_group(qg, kts, vts, biases, new_row, bias_new, new_on, sink):
    qb = qg.astype(BF16)
    ss = [jnp.dot(qb, kt, preferred_element_type=F32) + b for kt, b in zip(kts, biases)]
    s_new = jnp.sum(qg * new_row[:, :KVD], axis=-1, keepdims=True) + bias_new
    if new_on is not None:
        s_new = jnp.where(new_on, s_new, NEG)
    m = s_new
    for s in ss:
        m = jnp.maximum(m, jnp.max(s, axis=-1, keepdims=True))
    if sink is not None:
        m = jnp.maximum(m, sink)
    e_new = jnp.exp(s_new - m)
    den = e_new
    if sink is not None:
        den = den + jnp.exp(sink - m)
    o = e_new * new_row[:, KVD:]
    for s, vt in zip(ss, vts):
        e = jnp.exp(s - m)
        den = den + jnp.sum(e, axis=-1, keepdims=True)
        o = o + _nt_dot(e.astype(BF16), vt)
    return o / den


def _pair_values(o0, o1):
    lane = lax.broadcasted_iota(jnp.int32, (GROUP, LANES), 1)
    return jnp.where(lane < HEAD_DIM, o0, o1)


def _dec_attn_kernel(*refs, nb, use_sink):
    if use_sink:
        q_ref, kv_ref, new_ref, bias_ref, bnew_ref, sink_ref, o_ref = refs
    else:
        q_ref, kv_ref, new_ref, bias_ref, bnew_ref, o_ref = refs
    for b in range(nb):
        qgs = _decode_queries(q_ref[b].astype(F32))
        kt = kv_ref[b, :KVD, :].astype(BF16)
        vt = kv_ref[b, KVD:, :].astype(BF16)
        outs = [_decode_group(qgs[g], [kt], [vt], [bias_ref[g]], new_ref[b], bnew_ref[g], None,
                              sink_ref[g] if use_sink else None) for g in range(N_KV_HEADS)]
        o_ref[b] = _pair_values(outs[0], outs[1]).astype(o_ref.dtype)


def _dec_attn(q, buf_t, new, bias, bias_new, sink):
    ns, _, l = buf_t.shape
    nb = 8
    use_sink = sink is not None
    const = lambda a: pl.BlockSpec(a.shape, lambda i: (0,) * a.ndim)
    in_specs = [pl.BlockSpec((nb, GROUP, LANES), lambda i: (i, 0, 0)),
                pl.BlockSpec((nb, KV_ROW, l), lambda i: (i, 0, 0)),
                pl.BlockSpec((nb, 1, KV_ROW), lambda i: (i, 0, 0)),
                const(bias), const(bias_new)]
    args = [q, buf_t, new, bias, bias_new]
    if use_sink:
        in_specs.append(const(sink))
        args.append(sink)
    return pl.pallas_call(
        functools.partial(_dec_attn_kernel, nb=nb, use_sink=use_sink),
        grid=(ns // nb,),
        in_specs=in_specs,
        out_specs=pl.BlockSpec((nb, GROUP, LANES), lambda i: (i, 0, 0)),
        out_shape=jax.ShapeDtypeStruct((ns, GROUP, LANES), BF16),
        compiler_params=_params(("parallel",)),
        name="dec_attn",
    )(*args)


def _dec_cmp_topk_kernel(q_ref, ck_ref, cv_ref, o_ref, idx_ref, imp_ref, *, nb, nsp, qpos):
    t1 = jnp.full((1, 1), qpos, jnp.int32)
    negm, has_valid = _cmp_mask(t1, nsp)
    lane = lax.broadcasted_iota(jnp.int32, (GROUP, LANES), 1)
    for b in range(nb):
        q = q_ref[b].astype(F32) * ATTN_SCALE
        outs = []
        for g in range(N_KV_HEADS):
            qg = jnp.where((lane < HEAD_DIM) == (g == 0), q, 0.0).astype(BF16)
            p = _cmp_softmax(_nt_dot(qg, ck_ref[b]), negm, has_valid)
            outs.append(jnp.dot(p.astype(BF16), cv_ref[b], preferred_element_type=F32))
            ph = jnp.sum(p, axis=0, keepdims=True)
            imp_ref[N_KV_HEADS * b + g:N_KV_HEADS * b + g + 1, :] = ph[:, :nsp] + ph[:, nsp:]
        o_ref[b] = jnp.where(lane < HEAD_DIM, outs[0], outs[1]).astype(o_ref.dtype)
    rows = N_KV_HEADS * nb
    _, idx = _select_blocks(imp_ref[...], jnp.full((rows, 1), qpos, jnp.int32))
    idx_ref[0] = idx.astype(jnp.int32)


def _dec_cmp_topk(q, ck, cv, qpos):
    ns = q.shape[0]
    nb = 8
    nsp = ck.shape[1] // 2
    return pl.pallas_call(
        functools.partial(_dec_cmp_topk_kernel, nb=nb, nsp=nsp, qpos=qpos),
        grid=(ns // nb,),
        in_specs=[pl.BlockSpec((nb, GROUP, LANES), lambda i: (i, 0, 0)),
                  pl.BlockSpec((nb, 2 * nsp, KVD), lambda i: (i, 0, 0)),
                  pl.BlockSpec((nb, 2 * nsp, KVD), lambda i: (i, 0, 0))],
        out_specs=[pl.BlockSpec((nb, GROUP, LANES), lambda i: (i, 0, 0)),
                   pl.BlockSpec((1, N_KV_HEADS * nb, LANES), lambda i: (i, 0, 0))],
        out_shape=[jax.ShapeDtypeStruct((ns, GROUP, LANES), BF16),
                   jax.ShapeDtypeStruct((ns // nb, N_KV_HEADS * nb, LANES), jnp.int32)],
        scratch_shapes=[pltpu.VMEM((N_KV_HEADS * nb, nsp), F32)],
        compiler_params=_params(("parallel",)),
        name="dec_cmp_topk",
    )(q, ck, cv)


BLOCKS_PER_PAGE = PAGE_SIZE // SEL_BLOCK


def _sel_pick(idx_ref, n, g, k, n_past):
    blk = idx_ref[(n * N_KV_HEADS + g) * SEL_TOPK + k]
    return jnp.clip(blk, 0, n_past - 1), (blk >= 0) & (blk < n_past), blk == n_past


def _sel_copies(idx_ref, pt_ref, pool_ref, buf_ref, sem_ref, n, slot, n_pages):
    copies = []
    for g in range(N_KV_HEADS):
        for k in range(SEL_TOPK):
            blk, _, _ = _sel_pick(idx_ref, n, g, k, n_pages * BLOCKS_PER_PAGE)
            phys = pt_ref[n * n_pages + blk // BLOCKS_PER_PAGE]
            copies.append(pltpu.make_async_copy(pool_ref.at[phys], buf_ref.at[slot, g, k], sem_ref.at[slot, g, k]))
    return copies


def _dec_sel_kernel(idx_ref, pt_ref, q_ref, new_ref, bt_ref, bnew_ref, pool_ref, o_ref, buf_ref, sem_ref, *,
                    n_pages):
    n = pl.program_id(0)
    slot = n % 2
    n_past = n_pages * BLOCKS_PER_PAGE

    @pl.when(n == 0)
    def _():
        for cp in _sel_copies(idx_ref, pt_ref, pool_ref, buf_ref, sem_ref, 0, 0, n_pages):
            cp.start()

    @pl.when(n + 1 < pl.num_programs(0))
    def _():
        for cp in _sel_copies(idx_ref, pt_ref, pool_ref, buf_ref, sem_ref, n + 1, 1 - slot, n_pages):
            cp.start()

    copies = _sel_copies(idx_ref, pt_ref, pool_ref, buf_ref, sem_ref, n, slot, n_pages)
    lane_block = lax.broadcasted_iota(jnp.int32, (GROUP, LANES), 1) // SEL_BLOCK
    biases, new_on = [], []
    for g in range(N_KV_HEADS):
        has_new = False
        rows = []
        for k in range(SEL_TOPK):
            blk, cached, is_new = _sel_pick(idx_ref, n, g, k, n_past)
            want = jnp.where(cached, blk % BLOCKS_PER_PAGE, BLOCKS_PER_PAGE)
            row = bt_ref[blk // BLOCKS_PER_PAGE, GROUP * g:GROUP * (g + 1), :]
            rows.append(jnp.where(lane_block == want, row, NEG))
            has_new = has_new | is_new
        biases.append(rows)
        new_on.append(has_new)
    for cp in copies:
        cp.wait()
    qgs = _decode_queries(q_ref[0].astype(F32))
    outs = []
    for g in range(N_KV_HEADS):
        kts = [buf_ref[slot, g, k, :KVD, :].astype(BF16) for k in range(SEL_TOPK)]
        vts = [buf_ref[slot, g, k, KVD:, :].astype(BF16) for k in range(SEL_TOPK)]
        outs.append(_decode_group(qgs[g], kts, vts, biases[g], new_ref[0], bnew_ref[g], new_on[g], None))
    o_ref[0] = _pair_values(outs[0], outs[1]).astype(o_ref.dtype)


def _dec_sel(idx, page_table, q, new, bt, bias_new, pool_t):
    ns, n_pages = page_table.shape
    grid_spec = pltpu.PrefetchScalarGridSpec(
        num_scalar_prefetch=2,
        grid=(ns,),
        in_specs=[pl.BlockSpec((1, GROUP, LANES), lambda n, ix, pt: (n, 0, 0)),
                  pl.BlockSpec((1, 1, KV_ROW), lambda n, ix, pt: (n, 0, 0)),
                  pl.BlockSpec(bt.shape, lambda n, ix, pt: (0, 0, 0)),
                  pl.BlockSpec(bias_new.shape, lambda n, ix, pt: (0, 0, 0)),
                  pl.BlockSpec(memory_space=pl.ANY)],
        out_specs=pl.BlockSpec((1, GROUP, LANES), lambda n, ix, pt: (n, 0, 0)),
        scratch_shapes=[pltpu.VMEM((2, N_KV_HEADS, SEL_TOPK, KV_ROW, PAGE_SIZE), F32),
                        pltpu.SemaphoreType.DMA((2, N_KV_HEADS, SEL_TOPK))])
    return pl.pallas_call(
        functools.partial(_dec_sel_kernel, n_pages=n_pages),
        grid_spec=grid_spec,
        out_shape=jax.ShapeDtypeStruct((ns, GROUP, LANES), BF16),
        compiler_params=_params(("arbitrary",)),
        name="dec_sel",
    )(idx, page_table.reshape(-1), q, new, bt, bias_new, pool_t)


def _t5_bucket(dist):
    n = np.maximum(dist, 0)
    exact = N_BUCKETS // 2
    nf = np.maximum(n, exact).astype(np.float32)
    scaled = np.log(nf / np.float32(exact)) / np.float32(math.log(T5_MAX_DISTANCE / exact)) * np.float32(N_BUCKETS - exact)
    large = exact + scaled.astype(np.int32)
    return np.where(n < exact, n, np.minimum(large, N_BUCKETS - 1)).astype(np.int32)


def _bias_of(table, dist):
    onehot = (jnp.asarray(_t5_bucket(dist))[..., None] == jnp.arange(N_BUCKETS)).astype(F32)
    return jnp.einsum('...b,bh->...h', onehot, table.astype(F32), precision=lax.Precision.HIGHEST)


def _stack_heads(b):
    return b.reshape(N_KV_HEADS, GROUP * b.shape[1], b.shape[2])


def _band_bias(table, tq, window):
    dist = np.arange(tq)[:, None] - np.arange(window + tq)[None, :] + window
    b = jnp.transpose(_bias_of(table, dist), (2, 0, 1))
    return _stack_heads(jnp.where(((dist >= 0) & (dist <= window))[None], b, NEG))


def _near_bias(table, tq):
    assert tq + 1 >= T5_MAX_DISTANCE
    dist = np.arange(tq)[:, None] - np.arange(2 * tq)[None, :] + tq
    tab = table.astype(F32)
    b = jnp.transpose(_bias_of(table, dist) - tab[N_BUCKETS - 1], (2, 0, 1))
    return _stack_heads(jnp.where((dist >= 0)[None], b, NEG))


def _decode_bias(table, length, window):
    dist = length - np.arange(length)
    tab = table.astype(F32)
    b = jnp.where((dist <= window)[None], _bias_of(table, dist).T, NEG)
    return b.reshape(N_KV_HEADS, GROUP, length), tab[0].reshape(N_KV_HEADS, GROUP, 1)


def _stack_rows(v, tq):
    return jnp.broadcast_to(v.astype(F32).reshape(N_KV_HEADS, GROUP, 1, 1),
                            (N_KV_HEADS, GROUP, tq, LANES)).reshape(N_KV_HEADS, GROUP * tq, LANES)


def _pair_cols(w):
    lead = w.shape[:-1]
    w = w.reshape(lead + (N_KV_HEADS, GROUP, HEAD_DIM))
    return jnp.swapaxes(w, -3, -2).reshape(lead + (HD,))


def _pair_rows(w):
    w = w.reshape(N_KV_HEADS, GROUP, HEAD_DIM, w.shape[-1])
    return jnp.swapaxes(w, 0, 1).reshape(HD, w.shape[-1])


def _dim_token(x):
    nd = x.ndim
    x = jnp.transpose(x, tuple(range(nd - 4)) + (nd - 3, nd - 2, nd - 1, nd - 4))
    return x.reshape(x.shape[:nd - 4] + (KV_ROW, x.shape[-1]))


def _even_odd(comp, nsp):
    def half(x):
        return jnp.pad(x, ((0, 0), (0, nsp - x.shape[1]), (0, 0)))
    eo = jnp.concatenate([half(comp[:, 0::2]), half(comp[:, 1::2])], axis=1).astype(BF16)
    return eo[:, :, :KVD], eo[:, :, KVD:]


def _front_pad(x, rows):
    return jnp.pad(x, ((0, 0), (rows, 0), (0, 0)))


def kernel(x_prompt, x_sample, state_a_kv, cache_cmp_kv, cache_sel_kv, state_win_kv, page_table, c_prompt, c_sample, rel_bias_table, a_norm_g, a_w_ada, a_b_ada, a_w_in, a_sinks, a_w_out, kv_norm_g, kv_w_ada, kv_b_ada, kv_w_in, cmp_pos, cmp_w1, cmp_w2, b_norm_g, b_w_ada, b_b_ada, b_w_in, b_w_out, final_norm_g):
    n_p, seq, d = x_prompt.shape
    n_s = x_sample.shape[0]
    n_pages = page_table.shape[1]
    past = n_pages * PAGE_SIZE
    n_a, n_b = a_w_in.shape[0], b_w_in.shape[0]
    tq = Q_BLOCK
    assert x_sample.shape[1] == 1 and seq % 512 == 0 and n_s % 8 == 0
    assert seq // SEL_BLOCK >= SEL_TOPK and past // SEL_BLOCK >= SEL_TOPK
    assert state_a_kv.shape[2] == WINDOW_A and state_win_kv.shape[1] == WINDOW_B

    c_all = jnp.concatenate([c_prompt, jnp.zeros((8 - n_p, d), F32), c_sample], axis=0)
    ada_a = _ada(c_all, a_w_ada, a_b_ada)
    ada_b = _ada(c_all, b_w_ada, b_b_ada)
    ada_kv = _ada(c_all, kv_w_ada[None], kv_b_ada[None])

    def mods(ada, parts):
        pr = [ada[:n_p, None, i * d:(i + 1) * d] for i in range(parts)]
        sm = [ada[None, 8:8 + n_s, i * d:(i + 1) * d] for i in range(parts)]
        return pr, sm

    table = rel_bias_table
    xs = x_sample.reshape(1, n_s, d)
    xp = x_prompt

    bias_a = _band_bias(table, BAND_TQ, WINDOW_A)
    dbias_a, dnew = _decode_bias(table, WINDOW_A, WINDOW_A)
    a_new_p, a_new_s = [], []
    for i in range(n_a):
        w_in = a_w_in[i]
        ws = [_pair_cols(w_in[:, :HD]).astype(BF16), w_in[:, HD:HD + KV_ROW].astype(BF16),
              _pair_cols(w_in[:, HD + KV_ROW:]).astype(BF16)]
        outs = [(0, 0, HD, BF16), (1, 0, KV_ROW, F32), (1, 0, KVD, BF16), (1, KVD, KV_ROW, BF16), (2, 0, HD, BF16)]
        w_out = _pair_rows(a_w_out[i]).astype(BF16)
        (sh_p, sc_p, gt_p), (sh_s, sc_s, gt_s) = mods(ada_a[i], 3)
        q, kv, kb, vb, z = _mod_proj(xp, a_norm_g[i], sh_p, sc_p, ws, outs, 512)
        vb0, vb1 = _with_ones(vb)
        o = _band_attn(q, _front_pad(kb, WINDOW_A), _front_pad(vb0, WINDOW_A), _front_pad(vb1, WINDOW_A), bias_a,
                       _stack_rows(a_sinks[i], BAND_TQ), WINDOW_A)
        xp = _finish(xp, gt_p, z, [o], w_out)
        a_new_p.append(kv[:, seq - WINDOW_A:].reshape(n_p, WINDOW_A, 2, N_KV_HEADS, HEAD_DIM))
        q, kv, _, _, z = _mod_proj(xs, a_norm_g[i], sh_s, sc_s, ws, outs, 128)
        new = kv.reshape(n_s, 1, KV_ROW)
        o = _dec_attn(q.reshape(n_s, GROUP, LANES), _dim_token(state_a_kv[i]), new, dbias_a, dnew,
                      a_sinks[i].astype(F32).reshape(N_KV_HEADS, GROUP, 1))
        xs = _finish(xs, gt_s, z, [o.reshape(1, n_s, HD)], w_out)
        a_new_s.append(jnp.concatenate([state_a_kv[i][:, 1:], kv.reshape(n_s, 1, 2, N_KV_HEADS, HEAD_DIM)], axis=1))

    (sh_p, sc_p), (sh_s, sc_s) = mods(ada_kv[0], 2)
    w_kv = [kv_w_in.astype(BF16)]
    outs = [(0, KV_ROW * b, KV_ROW * (b + 1), F32) for b in range(N_BRANCH)]
    outs += [(0, KV_ROW + KVD * h, KV_ROW + KVD * (h + 1), BF16) for h in range(4)]
    cmp_p, sel_p, win_p, sel_kp, sel_vp, win_kp, win_vp = _mod_proj(xp, kv_norm_g, sh_p, sc_p, w_kv, outs, 512)
    cmp_s, sel_s, win_s = _mod_proj(xs, kv_norm_g, sh_s, sc_s, w_kv, outs[:N_BRANCH], 128)
    cmp_s, sel_s, win_s = (r.reshape(n_s, 1, KV_ROW) for r in (cmp_s, sel_s, win_s))

    eye_g = jnp.eye(N_KV_HEADS, dtype=F32)
    w1e = jnp.einsum('kbdh,gG->bkgdGh', cmp_w1, eye_g).reshape(
        HALF_ROWS, KVD, N_KV_HEADS * CMP_HIDDEN).astype(BF16)
    w2e = jnp.einsum('khd,gG->kghGd', cmp_w2, eye_g).reshape(2, N_KV_HEADS * CMP_HIDDEN, KVD).astype(BF16)
    pos_e = jnp.broadcast_to(cmp_pos[:, :, None, :], (CMP_BLOCK, 2, N_KV_HEADS, HEAD_DIM)).reshape(HALF_ROWS, KVD)
    comp_p = _compress(cmp_p, pos_e, w1e, w2e)
    comp_past = _compress_paged(_dim_token(cache_cmp_kv), page_table, pos_e, w1e, w2e)
    tail = jnp.pad(cmp_s, ((0, 0), (0, SEL_BLOCK - 1), (0, 0))).reshape(1, n_s * SEL_BLOCK, KV_ROW)
    comp_tail = _compress(tail, pos_e, w1e, w2e).reshape(n_s, SEL_BLOCK // CMP_BLOCK, KV_ROW)

    nsp_p = -(-(seq // SEL_BLOCK) // LANES) * LANES
    nsp_s = -(-(past // SEL_BLOCK + 1) // LANES) * LANES
    ck_p, cv_p = _even_odd(comp_p, nsp_p)
    n_sel_past = past // SEL_BLOCK
    fill = jnp.zeros((n_s, nsp_s - n_sel_past - 1, KV_ROW), F32)
    comp_s = jnp.concatenate([comp_past[:, :n_sel_past], comp_tail[:, 0:1], fill,
                              comp_past[:, n_sel_past:], comp_tail[:, 1:2], fill], axis=1).astype(BF16)
    ck_s, cv_s = comp_s[:, :, :KVD], comp_s[:, :, KVD:]

    sel_vp0, sel_vp1 = (_front_pad(v, tq) for v in _with_ones(sel_vp))
    win_vp0, win_vp1 = (_front_pad(v, WINDOW_B) for v in _with_ones(win_vp))
    sel_kp, win_kp = _front_pad(sel_kp, tq), _front_pad(win_kp, WINDOW_B)
    key_block = (jnp.arange(tq + seq) - tq) // SEL_BLOCK
    is_block = (key_block[:, None] == jnp.arange(nsp_p)[None, :]) & (jnp.arange(tq + seq) >= tq)[:, None]
    et = jnp.where(is_block, NEG, 0.0).astype(BF16)
    nb_sel = _near_bias(table, tq)
    bias_b = _band_bias(table, tq, WINDOW_B)
    dbias_b, _ = _decode_bias(table, WINDOW_B, WINDOW_B)
    win_buf_t = _dim_token(state_win_kv)
    kpos = np.arange(past).reshape(n_pages, PAGE_SIZE)
    bt = jnp.transpose(_bias_of(table, past - kpos), (0, 2, 1))
    sel_pool_t = _dim_token(cache_sel_kv)

    head_of = _PERM // HEAD_DIM
    e_np = np.zeros((LANES, N_BRANCH * HD), np.float32)
    for br in range(N_BRANCH):
        e_np[br * N_HEADS + head_of, br * HD + np.arange(HD)] = 1.0
    e_gl = jnp.asarray(e_np, dtype=BF16)

    y_p = y_s = None
    for j in range(n_b):
        w_in = b_w_in[j]
        w_z = _pair_cols(w_in[:, HD:HD + N_BRANCH * HD].reshape(d, N_BRANCH, HD)).reshape(d, N_BRANCH * HD)
        ws = [_pair_cols(w_in[:, :HD]).astype(BF16), w_z.astype(BF16),
              jnp.pad(w_in[:, HD + N_BRANCH * HD:], ((0, 0), (0, LANES - N_BRANCH * N_HEADS))).astype(BF16)]
        outs = [(0, 0, HD, BF16), (1, 0, N_BRANCH * HD, BF16), (2, 0, LANES, F32)]
        w_out = _pair_rows(b_w_out[j]).astype(BF16)
        fg = final_norm_g if j == n_b - 1 else None
        (sh_p, sc_p, gt_p), (sh_s, sc_s, gt_s) = mods(ada_b[j], 3)
        q, z, gl = _mod_proj(xp, b_norm_g[j], sh_p, sc_p, ws, outs, 256)
        o_cmp, sel = _cmp_topk(q, ck_p, cv_p)
        o_sel = _sel_attn(q, sel, sel_kp, sel_vp0, sel_vp1, et, nb_sel)
        o_win = _band_attn(q, win_kp, win_vp0, win_vp1, bias_b, None, WINDOW_B)
        r = _finish(xp, gt_p, z, [o_cmp, o_sel, o_win], w_out, gl, e_gl, fg, 256)
        xp, y_p = r if fg is not None else (r, None)
        q, z, gl = _mod_proj(xs, b_norm_g[j], sh_s, sc_s, ws, outs, 128)
        q = q.reshape(n_s, GROUP, LANES)
        o_cmp, idx = _dec_cmp_topk(q, ck_s, cv_s, past)
        idx = idx.reshape(n_s, N_KV_HEADS, LANES)[:, :, :SEL_TOPK].reshape(-1)
        o_sel = _dec_sel(idx, page_table, q, sel_s, bt, dnew, sel_pool_t)
        o_win = _dec_attn(q, win_buf_t, win_s, dbias_b, dnew, None)
        r = _finish(xs, gt_s, z, [o.reshape(1, n_s, HD) for o in (o_cmp, o_sel, o_win)], w_out, gl, e_gl, fg, 128)
        xs, y_s = r if fg is not None else (r, None)

    kv5 = lambda r: r.reshape(r.shape[0], r.shape[1], 2, N_KV_HEADS, HEAD_DIM)
    new_win_s = jnp.concatenate([state_win_kv[:, 1:], kv5(win_s)], axis=1)
    return (y_p, y_s.reshape(n_s, 1, d), jnp.stack(a_new_p), jnp.stack(a_new_s), kv5(cmp_p), kv5(cmp_s),
            kv5(sel_p), kv5(sel_s), kv5(win_p[:, seq - WINDOW_B:]), new_win_s)
```
